```python
import math
import jax
import jax.numpy as jnp
from jax import lax
import numpy as np

D_MODEL = 2048
BATCH = 8
SEQ = 2048
DEPTH = 1
DEC_BATCH = 2
DEC_SEQ = 8192
PAST_LEN = 128

N_HEADS = 4
DH = 128
ATTN_QK_WIDTH = N_HEADS * 2 * DH
ATTN_V_WIDTH = N_HEADS * 2 * DH
ROT_DIM = DH // 4
ROPE_THETA = 500000.0
Q_BLOCK = 128
ATTN_SCALE = DH ** -0.5
CHUNK = 128
SGU_GROUPS = 8
SGU_GROUP_DIM = 128
SGU_WIDTH = SGU_GROUPS * SGU_GROUP_DIM
D_FF = 4 * D_MODEL
SPLIT_SIZES = (ATTN_QK_WIDTH, ATTN_QK_WIDTH, ATTN_V_WIDTH, SGU_WIDTH, SGU_WIDTH, D_MODEL, D_MODEL)
SPLIT_POINTS = tuple(int(s) for s in np.cumsum(SPLIT_SIZES)[:-1])
IN_COLS = sum(SPLIT_SIZES)
RMS_EPS = 1e-6
LN_EPS = 1e-5

kernel_name = "hybrid_diffattn_sgu_encoder"


def lambda_init_fn(layer_idx):
    return 0.8 - 0.6 * math.exp(-0.3 * layer_idx)


def rmsnorm(x, g, eps=RMS_EPS):
    xf = x.astype(jnp.float32)
    y = xf * lax.rsqrt(jnp.mean(xf * xf, axis=-1, keepdims=True) + eps)
    return (y * g.astype(jnp.float32)).astype(x.dtype)


def layernorm(x, g, b, eps=LN_EPS):
    xf = x.astype(jnp.float32)
    mu = jnp.mean(xf, axis=-1, keepdims=True)
    xc = xf - mu
    y = xc * lax.rsqrt(jnp.mean(xc * xc, axis=-1, keepdims=True) + eps)
    return (y * g.astype(jnp.float32) + b.astype(jnp.float32)).astype(x.dtype)


def partial_rotary(t, seq_len):
    pos = jnp.arange(seq_len, dtype=jnp.float32)
    inv_freq = 1.0 / (jnp.float32(ROPE_THETA) ** (jnp.arange(0, ROT_DIM, 2, dtype=jnp.float32) / ROT_DIM))
    ang = pos[:, None] * inv_freq[None, :]
    cos = jnp.cos(ang)[None, :, None, None, :]
    sin = jnp.sin(ang)[None, :, None, None, :]
    tf = t.astype(jnp.float32)
    half = ROT_DIM // 2
    x1 = tf[..., :half]
    x2 = tf[..., half:ROT_DIM]
    rot = jnp.concatenate([x1 * cos - x2 * sin, x2 * cos + x1 * sin, tf[..., ROT_DIM:]], axis=-1)
    return rot.astype(t.dtype)


def diff_attention(q, k, v, lam):
    B, S = q.shape[0], q.shape[1]
    nb = S // Q_BLOCK
    qb = q.reshape(B, nb, Q_BLOCK, N_HEADS, 2, DH).transpose(1, 0, 2, 3, 4, 5)
    lam32 = lam.astype(jnp.float32)

    def one_block(qblk):
        s = jnp.einsum('bqhcd,bkhcd->bhcqk', qblk, k,
                       preferred_element_type=jnp.float32) * ATTN_SCALE
        p = jax.nn.softmax(s, axis=-1)
        w = (p[:, :, 0] - lam32 * p[:, :, 1]).astype(v.dtype)
        return jnp.einsum('bhqk,bkhe->bqhe', w, v)

    o = lax.map(one_block, qb)
    return o.transpose(1, 0, 2, 3, 4).reshape(B, S, N_HEADS, 2 * DH)


def spatial_gating(su, sv, ln_g, ln_b, sgu_w, sgu_b):
    B, S = su.shape[0], su.shape[1]
    sv = layernorm(sv, ln_g, ln_b)
    svc = sv.reshape(B, S // CHUNK, CHUNK, SGU_GROUPS, SGU_GROUP_DIM)
    mixed = jnp.einsum('gpq,bnqgc->bnpgc', sgu_w, svc) + sgu_b.T[:, :, None]
    return su * mixed.reshape(B, S, SGU_WIDTH)


def encoder_layer(x, layer_idx, attn_norm_g, w_in, lambda_q1, lambda_k1, lambda_q2, lambda_k2,
                  subln_g, sgu_ln_g, sgu_ln_b, sgu_w, sgu_b, w_br_attn, w_br_sgu, w_out,
                  mlp_norm_g, w_up, w_down):
    B, S = x.shape[0], x.shape[1]
    lambda_init = lambda_init_fn(layer_idx)
    h = rmsnorm(x, attn_norm_g)
    z = h @ w_in
    q, k, v, su, sv, g_attn, g_sgu = jnp.split(z, SPLIT_POINTS, axis=-1)

    q = partial_rotary(q.reshape(B, S, N_HEADS, 2, DH), S)
    k = partial_rotary(k.reshape(B, S, N_HEADS, 2, DH), S)
    v = v.reshape(B, S, N_HEADS, 2 * DH)
    lam = (jnp.exp(jnp.sum(lambda_q1.astype(jnp.float32) * lambda_k1.astype(jnp.float32)))
           - jnp.exp(jnp.sum(lambda_q2.astype(jnp.float32) * lambda_k2.astype(jnp.float32)))
           + lambda_init)
    o = diff_attention(q, k, v, lam)
    o_attn = (rmsnorm(o, subln_g) * (1.0 - lambda_init)).reshape(B, S, ATTN_V_WIDTH)

    o_sgu = spatial_gating(jax.nn.gelu(su, approximate=False), jax.nn.gelu(sv, approximate=False),
                           sgu_ln_g, sgu_ln_b, sgu_w, sgu_b)

    merged = jax.nn.sigmoid(g_attn) * (o_attn @ w_br_attn) + jax.nn.sigmoid(g_sgu) * (o_sgu @ w_br_sgu)
    x = x + merged @ w_out

    h2 = rmsnorm(x, mlp_norm_g)
    x = x + jnp.square(jax.nn.relu(h2 @ w_up)) @ w_down
    return x


def encoder_trunk(x, attn_norm_g, w_in, lambda_q1, lambda_k1, lambda_q2, lambda_k2, subln_g,
                  sgu_ln_g, sgu_ln_b, sgu_w, sgu_b, w_br_attn, w_br_sgu, w_out, mlp_norm_g,
                  w_up, w_down, final_norm_g):
    for l in range(DEPTH):
        x = encoder_layer(x, l, attn_norm_g[l], w_in[l], lambda_q1[l], lambda_k1[l], lambda_q2[l],
                          lambda_k2[l], subln_g[l], sgu_ln_g[l], sgu_ln_b[l], sgu_w[l], sgu_b[l],
                          w_br_attn[l], w_br_sgu[l], w_out[l], mlp_norm_g[l], w_up[l], w_down[l])
    return rmsnorm(x, final_norm_g)


def setup_inputs(seed: int = 0) -> dict:
    key = jax.random.key(seed)
    ks = jax.random.split(key, 22)
    f32 = jnp.float32
    n = lambda k, shape, s: jax.random.normal(k, shape, f32) * s
    return {
        "x_prompt": jax.random.normal(ks[0], (BATCH, SEQ, D_MODEL), f32),
        "x_sample": jax.random.normal(ks[1], (DEC_BATCH, DEC_SEQ, D_MODEL), f32),
        "attn_norm_g": 1.0 + n(ks[2], (DEPTH, D_MODEL), 0.02),
        "w_in": n(ks[3], (DEPTH, D_MODEL, IN_COLS), D_MODEL ** -0.5),
        "lambda_q1": n(ks[4], (DEPTH, DH), 0.1),
        "lambda_k1": n(ks[5], (DEPTH, DH), 0.1),
        "lambda_q2": n(ks[6], (DEPTH, DH), 0.1),
        "lambda_k2": n(ks[7], (DEPTH, DH), 0.1),
        "subln_g": 1.0 + n(ks[8], (DEPTH, 2 * DH), 0.02),
        "sgu_ln_g": 1.0 + n(ks[9], (DEPTH, SGU_WIDTH), 0.02),
        "sgu_ln_b": n(ks[10], (DEPTH, SGU_WIDTH), 0.02),
        "sgu_w": n(ks[11], (DEPTH, SGU_GROUPS, CHUNK, CHUNK), CHUNK ** -0.5),
        "sgu_b": n(ks[12], (DEPTH, SGU_GROUPS, CHUNK), 0.02),
        "w_br_attn": n(ks[13], (DEPTH, ATTN_V_WIDTH, D_MODEL), ATTN_V_WIDTH ** -0.5),
        "w_br_sgu": n(ks[14], (DEPTH, SGU_WIDTH, D_MODEL), SGU_WIDTH ** -0.5),
        "w_out": n(ks[15], (DEPTH, D_MODEL, D_MODEL), D_MODEL ** -0.5),
        "mlp_norm_g": 1.0 + n(ks[16], (DEPTH, D_MODEL), 0.02),
        "w_up": n(ks[17], (DEPTH, D_MODEL, D_FF), D_MODEL ** -0.5),
        "w_down": n(ks[18], (DEPTH, D_FF, D_MODEL), D_FF ** -0.5),
        "final_norm_g": 1.0 + n(ks[19], (D_MODEL,), 0.02),
    }


def reference(x_prompt, x_sample, attn_norm_g, w_in, lambda_q1, lambda_k1, lambda_q2, lambda_k2,
              subln_g, sgu_ln_g, sgu_ln_b, sgu_w, sgu_b, w_br_attn, w_br_sgu, w_out, mlp_norm_g,
              w_up, w_down, final_norm_g):
    y_prompt = encoder_trunk(x_prompt, attn_norm_g, w_in, lambda_q1, lambda_k1, lambda_q2, lambda_k2,
                             subln_g, sgu_ln_g, sgu_ln_b, sgu_w, sgu_b, w_br_attn, w_br_sgu, w_out,
                             mlp_norm_g, w_up, w_down, final_norm_g)
    y_sample = encoder_trunk(x_sample, attn_norm_g, w_in, lambda_q1, lambda_k1, lambda_q2, lambda_k2,
                             subln_g, sgu_ln_g, sgu_ln_b, sgu_w, sgu_b, w_br_attn, w_br_sgu, w_out,
                             mlp_norm_g, w_up, w_down, final_norm_g)
    return (y_prompt, y_sample)
```

```python
import functools
import math

import jax
import jax.numpy as jnp
from jax import lax
from jax.experimental import pallas as pl
from jax.experimental.pallas import tpu as pltpu

D_MODEL = 2048
N_HEADS = 4
DH = 128
HEAD_W = 2 * DH
ATTN_W = N_HEADS * HEAD_W
ROT_DIM = DH // 4
ROPE_THETA = 500000.0
ATTN_SCALE = DH ** -0.5
CHUNK = 128
SGU_GROUPS = 8
SGU_W = SGU_GROUPS * CHUNK
D_FF = 4 * D_MODEL
ZCOLS = 3 * ATTN_W + 2 * SGU_W
GATE_A_COL = ZCOLS
GATE_S_COL = ZCOLS + D_MODEL
RMS_EPS = 1e-6
LN_EPS = 1e-5
SQRT_HALF = math.sqrt(0.5)
EXP2_SCALE = ATTN_SCALE * math.log2(math.e)

VMEM_LIMIT_BYTES_V7X = 56 * 1024 * 1024

MERGE_TN = 512
MLP_TF = 1024

F32 = jnp.float32
BF16 = jnp.bfloat16


def _lambda_init(layer_idx):
    return 0.8 - 0.6 * math.exp(-0.3 * layer_idx)


def _params(semantics):
    return pltpu.CompilerParams(dimension_semantics=semantics,
                                vmem_limit_bytes=VMEM_LIMIT_BYTES_V7X)


def _rms(x, g):
    ms = jnp.mean(x * x, axis=-1, keepdims=True)
    return x * lax.rsqrt(ms + RMS_EPS) * g


def _gelu(x):
    return 0.5 * x * (1.0 + lax.erf(x * SQRT_HALF))


def _in_proj_kernel(x_ref, g_ref, w_ref, rc_ref, ra_ref, rb_ref, lng_ref, lnb_ref,
                    z_ref, h_ref):
    j = pl.program_id(1)

    @pl.when(j == 0)
    def _():
        h_ref[...] = _rms(x_ref[...], g_ref[...]).astype(BF16)

    acc = jnp.dot(h_ref[...], w_ref[...], preferred_element_type=F32)

    @pl.when(j < 2)
    def _():
        rc, ra, rb = rc_ref[...], ra_ref[...], rb_ref[...]
        for c in range(ATTN_W // DH):
            t = acc[:, c * DH:(c + 1) * DH]
            r = (t * rc + pltpu.roll(t, DH - ROT_DIM // 2, 1) * ra
                 + pltpu.roll(t, ROT_DIM // 2, 1) * rb)
            z_ref[:, c * DH:(c + 1) * DH] = r.astype(BF16)

    @pl.when(j == 2)
    def _():
        z_ref[...] = acc.astype(BF16)

    @pl.when(j == 3)
    def _():
        z_ref[...] = _gelu(acc).astype(BF16)

    @pl.when(j == 4)
    def _():
        y = _gelu(acc)
        mu = jnp.mean(y, axis=-1, keepdims=True)
        yc = y - mu
        var = jnp.mean(yc * yc, axis=-1, keepdims=True)
        z_ref[...] = (yc * lax.rsqrt(var + LN_EPS) * lng_ref[...] + lnb_ref[...]).astype(BF16)


def _in_proj(x2, seq_len, norm_g, w_in_bf, rot, ln_g, ln_b, tm):
    T = x2.shape[0]
    n_pos_blocks = seq_len // tm
    rot_spec = pl.BlockSpec((tm, DH), lambda i, j: (i % n_pos_blocks, 0))
    vec = lambda n: pl.BlockSpec((1, n), lambda i, j: (0, 0))
    return pl.pallas_call(
        _in_proj_kernel,
        grid=(T // tm, ZCOLS // ATTN_W),
        in_specs=[
            pl.BlockSpec((tm, D_MODEL), lambda i, j: (i, 0)),
            vec(D_MODEL),
            pl.BlockSpec((D_MODEL, ATTN_W), lambda i, j: (0, j)),
            rot_spec, rot_spec, rot_spec,
            vec(SGU_W), vec(SGU_W),
        ],
        out_specs=pl.BlockSpec((tm, ATTN_W), lambda i, j: (i, j)),
        out_shape=jax.ShapeDtypeStruct((T, ZCOLS), BF16),
        scratch_shapes=[pltpu.VMEM((tm, D_MODEL), BF16)],
        compiler_params=_params(("parallel", "arbitrary")),
    )(x2, norm_g, w_in_bf, rot[0], rot[1], rot[2], ln_g, ln_b)


def _attn_kernel(q_ref, k_ref, v_ref, lq1_ref, lk1_ref, lq2_ref, lk2_ref, g_ref, o_ref,
                 *, tk, n_kv, lambda_init):
    tq = q_ref.shape[0]
    q = q_ref[...]
    qs = (q[:, :DH], q[:, DH:])

    def body(j, carry):
        start = pl.multiple_of(j * tk, tk)
        v = v_ref[pl.ds(start, tk), :]
        new = []
        for c in range(2):
            m, l, acc = carry[c]
            k = k_ref[pl.ds(start, tk), c * DH:(c + 1) * DH]
            s = lax.dot_general(qs[c], k, (((1,), (1,)), ((), ())),
                                preferred_element_type=F32)
            m_new = jnp.maximum(m, jnp.max(s, axis=-1, keepdims=True))
            alpha = jnp.exp2((m - m_new) * EXP2_SCALE)
            p = jnp.exp2((s - m_new) * EXP2_SCALE)
            l = alpha * l + jnp.sum(p, axis=-1, keepdims=True)
            acc = alpha * acc + jnp.dot(p.astype(BF16), v, preferred_element_type=F32)
            new.append((m_new, l, acc))
        return tuple(new)

    init = tuple((jnp.full((tq, 1), -jnp.inf, F32), jnp.zeros((tq, 1), F32),
                  jnp.zeros((tq, HEAD_W), F32)) for _ in range(2))
    (_, l1, a1), (_, l2, a2) = lax.fori_loop(0, n_kv, body, init)

    lam = (jnp.exp(jnp.sum(lq1_ref[...] * lk1_ref[...], axis=-1, keepdims=True))
           - jnp.exp(jnp.sum(lq2_ref[...] * lk2_ref[...], axis=-1, keepdims=True))
           + lambda_init)
    o = a1 / l1 - lam * (a2 / l2)
    o_ref[...] = (_rms(o, g_ref[...]) * (1.0 - lambda_init)).astype(BF16)


def _attention(z3, lq1, lk1, lq2, lk2, subln_g, lambda_init, tq, tk):
    B, S, _ = z3.shape
    vec = lambda n: pl.BlockSpec((1, n), lambda b, h, i: (0, 0))
    kern = functools.partial(_attn_kernel, tk=tk, n_kv=S // tk, lambda_init=lambda_init)
    k_blk0 = ATTN_W // HEAD_W
    v_blk0 = 2 * ATTN_W // HEAD_W
    return pl.pallas_call(
        kern,
        grid=(B, N_HEADS, S // tq),
        in_specs=[
            pl.BlockSpec((None, tq, HEAD_W), lambda b, h, i: (b, i, h)),
            pl.BlockSpec((None, S, HEAD_W), lambda b, h, i: (b, 0, k_blk0 + h)),
            pl.BlockSpec((None, S, HEAD_W), lambda b, h, i: (b, 0, v_blk0 + h)),
            vec(DH), vec(DH), vec(DH), vec(DH),
            vec(HEAD_W),
        ],
        out_specs=pl.BlockSpec((None, tq, HEAD_W), lambda b, h, i: (b, i, h)),
        out_shape=jax.ShapeDtypeStruct((B, S, ATTN_W), BF16),
        compiler_params=_params(("parallel", "parallel", "arbitrary")),
    )(z3, z3, z3, lq1, lk1, lq2, lk2, subln_g)


def _merge_kernel(x_ref, su_ref, sv_ref, oa_ref, g_ref, wga_ref, wgs_ref, wba_ref, wbs_ref,
                  wo_ref, sw_ref, sbt_ref, y_ref, h_ref, osgu_ref, acc_ref):
    j = pl.program_id(1)
    tm = x_ref.shape[0]

    @pl.when(j == 0)
    def _():
        h_ref[...] = _rms(x_ref[...], g_ref[...]).astype(BF16)
        for r in range(tm // CHUNK):
            rows = slice(r * CHUNK, (r + 1) * CHUNK)
            for g in range(SGU_GROUPS):
                cols = slice(g * CHUNK, (g + 1) * CHUNK)
                mixed = jnp.dot(sw_ref[g], sv_ref[rows, cols], preferred_element_type=F32)
                mixed = mixed + sbt_ref[:, g:g + 1]
                osgu_ref[rows, cols] = (su_ref[rows, cols].astype(F32) * mixed).astype(BF16)

    h = h_ref[...]
    ga = jax.nn.sigmoid(jnp.dot(h, wga_ref[...], preferred_element_type=F32))
    gs = jax.nn.sigmoid(jnp.dot(h, wgs_ref[...], preferred_element_type=F32))
    ba = jnp.dot(oa_ref[...], wba_ref[...], preferred_element_type=F32)
    bs = jnp.dot(osgu_ref[...], wbs_ref[...], preferred_element_type=F32)
    merged = (ga * ba + gs * bs).astype(BF16)
    part = jnp.dot(merged, wo_ref[...], preferred_element_type=F32)

    @pl.when(j == 0)
    def _():
        acc_ref[...] = part

    @pl.when(j > 0)
    def _():
        acc_ref[...] += part

    @pl.when(j == pl.num_programs(1) - 1)
    def _():
        y_ref[...] = x_ref[...] + acc_ref[...]


def _merge(x2, z, oa, norm_g, w_in_bf, wba, wbs, wo, sgu_w_bf, sgu_bt, tm, tn):
    T = x2.shape[0]
    su_blk = 3 * ATTN_W // SGU_W
    return pl.pallas_call(
        _merge_kernel,
        grid=(T // tm, D_MODEL // tn),
        in_specs=[
            pl.BlockSpec((tm, D_MODEL), lambda i, j: (i, 0)),
            pl.BlockSpec((tm, SGU_W), lambda i, j: (i, su_blk)),
            pl.BlockSpec((tm, SGU_W), lambda i, j: (i, su_blk + 1)),
            pl.BlockSpec((tm, ATTN_W), lambda i, j: (i, 0)),
            pl.BlockSpec((1, D_MODEL), lambda i, j: (0, 0)),
            pl.BlockSpec((D_MODEL, tn), lambda i, j: (0, GATE_A_COL // tn + j)),
            pl.BlockSpec((D_MODEL, tn), lambda i, j: (0, GATE_S_COL // tn + j)),
            pl.BlockSpec((ATTN_W, tn), lambda i, j: (0, j)),
            pl.BlockSpec((SGU_W, tn), lambda i, j: (0, j)),
            pl.BlockSpec((tn, D_MODEL), lambda i, j: (j, 0)),
            pl.BlockSpec((SGU_GROUPS, CHUNK, CHUNK), lambda i, j: (0, 0, 0)),
            pl.BlockSpec((CHUNK, SGU_GROUPS), lambda i, j: (0, 0)),
        ],
        out_specs=pl.BlockSpec((tm, D_MODEL), lambda i, j: (i, 0)),
        out_shape=jax.ShapeDtypeStruct((T, D_MODEL), F32),
        scratch_shapes=[pltpu.VMEM((tm, D_MODEL), BF16),
                        pltpu.VMEM((tm, SGU_W), BF16),
                        pltpu.VMEM((tm, D_MODEL), F32)],
        compiler_params=_params(("parallel", "arbitrary")),
    )(x2, z, z, oa, norm_g, w_in_bf, w_in_bf, wba, wbs, wo, sgu_w_bf, sgu_bt)


def _mlp_kernel(x_ref, g_ref, wu_ref, wd_ref, fg_ref, y_ref, h_ref, acc_ref, *, final_norm):
    j = pl.program_id(1)

    @pl.when(j == 0)
    def _():
        h_ref[...] = _rms(x_ref[...], g_ref[...]).astype(BF16)

    u = jnp.dot(h_ref[...], wu_ref[...], preferred_element_type=F32)
    u = jnp.square(jnp.maximum(u, 0.0)).astype(BF16)
    part = jnp.dot(u, wd_ref[...], preferred_element_type=F32)

    @pl.when(j == 0)
    def _():
        acc_ref[...] = part

    @pl.when(j > 0)
    def _():
        acc_ref[...] += part

    @pl.when(j == pl.num_programs(1) - 1)
    def _():
        y = x_ref[...] + acc_ref[...]
        if final_norm:
            y = _rms(y, fg_ref[...])
        y_ref[...] = y


def _mlp(x2, norm_g, wu, wd, final_g, final_norm, tm, tf):
    T = x2.shape[0]
    return pl.pallas_call(
        functools.partial(_mlp_kernel, final_norm=final_norm),
        grid=(T // tm, D_FF // tf),
        in_specs=[
            pl.BlockSpec((tm, D_MODEL), lambda i, j: (i, 0)),
            pl.BlockSpec((1, D_MODEL), lambda i, j: (0, 0)),
            pl.BlockSpec((D_MODEL, tf), lambda i, j: (0, j)),
            pl.BlockSpec((tf, D_MODEL), lambda i, j: (j, 0)),
            pl.BlockSpec((1, D_MODEL), lambda i, j: (0, 0)),
        ],
        out_specs=pl.BlockSpec((tm, D_MODEL), lambda i, j: (i, 0)),
        out_shape=jax.ShapeDtypeStruct((T, D_MODEL), F32),
        scratch_shapes=[pltpu.VMEM((tm, D_MODEL), BF16),
                        pltpu.VMEM((tm, D_MODEL), F32)],
        compiler_params=_params(("parallel", "arbitrary")),
    )(x2, norm_g, wu, wd, final_g)


def _rot_tables(seq_len):
    pos = jnp.arange(seq_len, dtype=F32)
    inv_freq = 1.0 / (jnp.float32(ROPE_THETA) ** (jnp.arange(0, ROT_DIM, 2, dtype=F32) / ROT_DIM))
    ang = pos[:, None] * inv_freq[None, :]
    cos, sin = jnp.cos(ang), jnp.sin(ang)
    half = ROT_DIM // 2
    z = lambda n: jnp.zeros((seq_len, n), F32)
    rc = jnp.concatenate([cos, cos, jnp.ones((seq_len, DH - ROT_DIM), F32)], axis=-1)
    ra = jnp.concatenate([-sin, z(DH - half)], axis=-1)
    rb = jnp.concatenate([z(half), sin, z(DH - ROT_DIM)], axis=-1)
    return rc, ra, rb


def _tiles(seq_len):
    tm = min(512, seq_len)
    tq = min(512, seq_len)
    tk = min(512, seq_len)
    return tm, tq, tk


def _prepare(p):
    row = lambda a: a.reshape(1, -1).astype(F32)
    layers = []
    for l in range(p["w_in"].shape[0]):
        layers.append(dict(
            attn_norm_g=row(p["attn_norm_g"][l]), w_in=p["w_in"][l].astype(BF16),
            lq1=row(p["lambda_q1"][l]), lk1=row(p["lambda_k1"][l]),
            lq2=row(p["lambda_q2"][l]), lk2=row(p["lambda_k2"][l]),
            subln_g=row(p["subln_g"][l]), ln_g=row(p["sgu_ln_g"][l]), ln_b=row(p["sgu_ln_b"][l]),
            sgu_w=p["sgu_w"][l].astype(BF16), sgu_bt=p["sgu_b"][l].T.astype(F32),
            w_br_attn=p["w_br_attn"][l].astype(BF16), w_br_sgu=p["w_br_sgu"][l].astype(BF16),
            w_out=p["w_out"][l].astype(BF16), mlp_norm_g=row(p["mlp_norm_g"][l]),
            w_up=p["w_up"][l].astype(BF16), w_down=p["w_down"][l].astype(BF16)))
    return layers, row(p["final_norm_g"])


def _trunk(x, layers, final_g):
    B, S, _ = x.shape
    tm, tq, tk = _tiles(S)
    rot = _rot_tables(S)
    x2 = x.reshape(B * S, D_MODEL)
    for l, w in enumerate(layers):
        z = _in_proj(x2, S, w["attn_norm_g"], w["w_in"], rot, w["ln_g"], w["ln_b"], tm)
        oa = _attention(z.reshape(B, S, ZCOLS), w["lq1"], w["lk1"], w["lq2"], w["lk2"],
                        w["subln_g"], _lambda_init(l), tq, tk)
        x2 = _merge(x2, z, oa.reshape(B * S, ATTN_W), w["attn_norm_g"], w["w_in"],
                    w["w_br_attn"], w["w_br_sgu"], w["w_out"], w["sgu_w"], w["sgu_bt"],
                    tm, MERGE_TN)
        x2 = _mlp(x2, w["mlp_norm_g"], w["w_up"], w["w_down"], final_g,
                  l == len(layers) - 1, tm, MLP_TF)
    return x2.reshape(B, S, D_MODEL)


def kernel(x_prompt, x_sample, attn_norm_g, w_in, lambda_q1, lambda_k1, lambda_q2, lambda_k2,
           subln_g, sgu_ln_g, sgu_ln_b, sgu_w, sgu_b, w_br_attn, w_br_sgu, w_out, mlp_norm_g,
           w_up, w_down, final_norm_g):
    p = dict(attn_norm_g=attn_norm_g, w_in=w_in, lambda_q1=lambda_q1, lambda_k1=lambda_k1,
             lambda_q2=lambda_q2, lambda_k2=lambda_k2, subln_g=subln_g, sgu_ln_g=sgu_ln_g,
             sgu_ln_b=sgu_ln_b, sgu_w=sgu_w, sgu_b=sgu_b, w_br_attn=w_br_attn, w_br_sgu=w_br_sgu,
             w_out=w_out, mlp_norm_g=mlp_norm_g, w_up=w_up, w_down=w_down,
             final_norm_g=final_norm_g)
    layers, final_g = _prepare(p)
    return _trunk(x_prompt, layers, final_g), _trunk(x_sample, layers, final_g)
```

```python
import functools
import math

import numpy as np
import jax
import jax.numpy as jnp
from jax import lax
from jax.experimental import pallas as pl
from jax.experimental.pallas import tpu as pltpu

D_MODEL = 2048
N_HEADS = 4
DH = 128
HEAD_W = 2 * DH
ATTN_W = N_HEADS * HEAD_W
ROT_DIM = DH // 4
ROT_HALF = ROT_DIM // 2
ROPE_THETA = 500000.0
ATTN_SCALE = DH ** -0.5
CHUNK = 128
SGU_GROUPS = 8
SGU_W = SGU_GROUPS * CHUNK
D_FF = 4 * D_MODEL
ZCOLS = 3 * ATTN_W + 2 * SGU_W
GATE_A_COL = ZCOLS
GATE_S_COL = ZCOLS + D_MODEL
IN_COLS = ZCOLS + 2 * D_MODEL
RMS_EPS = 1e-6
LN_EPS = 1e-5
SQRT_HALF = math.sqrt(0.5)
EXP2_SCALE = ATTN_SCALE * math.log2(math.e)

VMEM_LIMIT_BYTES_V7X = 56 * 1024 * 1024

MERGE_TN = 512
MLP_TF = 1024

F32 = jnp.float32
BF16 = jnp.bfloat16


def _lambda_init(layer_idx):
    return 0.8 - 0.6 * math.exp(-0.3 * layer_idx)


def _params(semantics):
    return pltpu.CompilerParams(dimension_semantics=semantics,
                                vmem_limit_bytes=VMEM_LIMIT_BYTES_V7X)


def _rms(x, g):
    ms = jnp.mean(x * x, axis=-1, keepdims=True)
    return x * lax.rsqrt(ms + RMS_EPS) * g


def _gelu(x):
    return 0.5 * x * (1.0 + lax.erf(x * SQRT_HALF))


def _in_proj_kernel(x_ref, g_ref, w_ref, rc_ref, rs_ref, lng_ref, lnb_ref, z_ref, h_ref):
    j = pl.program_id(1)

    @pl.when(j == 0)
    def _():
        h_ref[...] = _rms(x_ref[...], g_ref[...]).astype(BF16)

    acc = jnp.dot(h_ref[...], w_ref[...], preferred_element_type=F32)

    @pl.when(j < 2)
    def _():
        rc, rs = rc_ref[...], rs_ref[...]
        for c in range(ATTN_W // DH):
            t = acc[:, c * DH:(c + 1) * DH]
            z_ref[:, c * DH:(c + 1) * DH] = (t * rc + pltpu.roll(t, DH // 2, 1) * rs).astype(BF16)

    @pl.when(j == 2)
    def _():
        z_ref[...] = acc.astype(BF16)

    @pl.when(j == 3)
    def _():
        z_ref[...] = _gelu(acc).astype(BF16)

    @pl.when(j == 4)
    def _():
        y = _gelu(acc)
        mu = jnp.mean(y, axis=-1, keepdims=True)
        yc = y - mu
        var = jnp.mean(yc * yc, axis=-1, keepdims=True)
        z_ref[...] = (yc * lax.rsqrt(var + LN_EPS) * lng_ref[...] + lnb_ref[...]).astype(BF16)


def _in_proj(x2, seq_len, norm_g, w_in_bf, rot, ln_g, ln_b, tm):
    T = x2.shape[0]
    n_pos_blocks = seq_len // tm
    rot_spec = pl.BlockSpec((None, tm, DH), lambda i, j: (jnp.minimum(j, 1), i % n_pos_blocks, 0))
    vec = lambda n: pl.BlockSpec((1, n), lambda i, j: (0, 0))
    return pl.pallas_call(
        _in_proj_kernel,
        grid=(T // tm, ZCOLS // ATTN_W),
        in_specs=[
            pl.BlockSpec((tm, D_MODEL), lambda i, j: (i, 0)),
            vec(D_MODEL),
            pl.BlockSpec((D_MODEL, ATTN_W), lambda i, j: (0, j)),
            rot_spec, rot_spec,
            vec(SGU_W), vec(SGU_W),
        ],
        out_specs=pl.BlockSpec((tm, ATTN_W), lambda i, j: (i, j)),
        out_shape=jax.ShapeDtypeStruct((T, ZCOLS), BF16),
        scratch_shapes=[pltpu.VMEM((tm, D_MODEL), BF16)],
        compiler_params=_params(("parallel", "arbitrary")),
    )(x2, norm_g, w_in_bf, rot[0], rot[1], ln_g, ln_b)


def _attn_kernel(q_ref, k_ref, v_ref, lq1_ref, lk1_ref, lq2_ref, lk2_ref, g_ref, o_ref,
                 s0_ref, s1_ref, p0_ref, p1_ref, acc_ref, *, tk, n_kv, lambda_init):
    s_refs, p_refs = (s0_ref, s1_ref), (p0_ref, p1_ref)
    q = q_ref[...]

    def chunk_rows(j):
        return pl.ds(pl.multiple_of(j * tk, tk), tk)

    def scores(j, slot):
        rows = chunk_rows(j)
        col_max = []
        for c in range(2):
            s = lax.dot_general(k_ref[rows, c * DH:(c + 1) * DH], q[:, c * DH:(c + 1) * DH],
                                (((1,), (1,)), ((), ())), preferred_element_type=F32)
            s_refs[slot][c] = s
            col_max.append(jnp.max(s, axis=0, keepdims=True))
        return tuple(col_max)

    def softmax(slot, col_max, stats):
        new_stats, alphas = [], []
        for c in range(2):
            m, l = stats[c]
            m_new = jnp.maximum(m, col_max[c])
            alpha = jnp.exp2(m - m_new)
            p = jnp.exp2(s_refs[slot][c] - m_new)
            p_refs[slot][c] = p.astype(BF16)
            new_stats.append((m_new, alpha * l + jnp.sum(p, axis=0, keepdims=True)))
            alphas.append(alpha)
        return tuple(new_stats), tuple(alphas)

    def values(j, slot, alphas):
        v = v_ref[chunk_rows(j), :]
        for c in range(2):
            pv = lax.dot_general(v, p_refs[slot][c], (((0,), (0,)), ((), ())),
                                 preferred_element_type=F32)
            acc_ref[c] = alphas[c] * acc_ref[c] + pv

    tq = q_ref.shape[0]
    acc_ref[...] = jnp.zeros(acc_ref.shape, F32)
    stats = tuple((jnp.full((1, tq), -jnp.inf, F32), jnp.zeros((1, tq), F32)) for _ in range(2))

    col_max = scores(0, 0)
    col_max_next = scores(1, 1)
    stats, alphas = softmax(0, col_max, stats)

    def two_chunks(u, carry):
        col_max, stats, alphas = carry
        col_max_a = scores(2 * u + 2, 0)
        stats, alphas_a = softmax(1, col_max, stats)
        values(2 * u, 0, alphas)
        col_max_b = scores(2 * u + 3, 1)
        stats, alphas_b = softmax(0, col_max_a, stats)
        values(2 * u + 1, 1, alphas_a)
        return col_max_b, stats, alphas_b

    col_max, stats, alphas = lax.fori_loop(0, n_kv // 2 - 1, two_chunks,
                                           (col_max_next, stats, alphas))
    stats, alphas_last = softmax(1, col_max, stats)
    values(n_kv - 2, 0, alphas)
    values(n_kv - 1, 1, alphas_last)

    (_, l1), (_, l2) = stats
    lam = (jnp.exp(jnp.sum(lq1_ref[...] * lk1_ref[...], axis=-1, keepdims=True))
           - jnp.exp(jnp.sum(lq2_ref[...] * lk2_ref[...], axis=-1, keepdims=True))
           + lambda_init)
    o_t = acc_ref[0] * (1.0 / l1) - lam * (acc_ref[1] * (1.0 / l2))
    ms = jnp.mean(o_t * o_t, axis=0, keepdims=True)
    y = (o_t * lax.rsqrt(ms + RMS_EPS)).T
    o_ref[...] = (y * g_ref[...] * (1.0 - lambda_init)).astype(BF16)


def _attention(z3, lq1, lk1, lq2, lk2, subln_g, lambda_init, tq, tk):
    B, S, _ = z3.shape
    n_kv = S // tk
    assert n_kv >= 2 and n_kv % 2 == 0, "the chunk pipeline takes an even number of key chunks"
    vec = lambda n: pl.BlockSpec((1, n), lambda b, h, i: (0, 0))
    kern = functools.partial(_attn_kernel, tk=tk, n_kv=n_kv, lambda_init=lambda_init)
    k_blk0 = ATTN_W // HEAD_W
    v_blk0 = 2 * ATTN_W // HEAD_W
    slot = lambda dt: pltpu.VMEM((2, tk, tq), dt)
    return pl.pallas_call(
        kern,
        grid=(B, N_HEADS, S // tq),
        in_specs=[
            pl.BlockSpec((None, tq, HEAD_W), lambda b, h, i: (b, i, h)),
            pl.BlockSpec((None, S, HEAD_W), lambda b, h, i: (b, 0, k_blk0 + h)),
            pl.BlockSpec((None, S, HEAD_W), lambda b, h, i: (b, 0, v_blk0 + h)),
            vec(DH), vec(DH), vec(DH), vec(DH),
            vec(HEAD_W),
        ],
        out_specs=pl.BlockSpec((None, tq, HEAD_W), lambda b, h, i: (b, i, h)),
        out_shape=jax.ShapeDtypeStruct((B, S, ATTN_W), BF16),
        scratch_shapes=[slot(F32), slot(F32), slot(BF16), slot(BF16),
                        pltpu.VMEM((2, HEAD_W, tq), F32)],
        compiler_params=_params(("parallel", "parallel", "arbitrary")),
    )(z3, z3, z3, lq1, lk1, lq2, lk2, subln_g)


def _merge_kernel(x_ref, su_ref, sv_ref, oa_ref, g_ref, wga_ref, wgs_ref, wba_ref, wbs_ref,
                  wo_ref, sw_ref, sbt_ref, y_ref, h_ref, osgu_ref, acc_ref):
    j = pl.program_id(1)
    tm = x_ref.shape[0]

    @pl.when(j == 0)
    def _():
        acc_ref[...] = jnp.zeros(acc_ref.shape, F32)
        h_ref[...] = _rms(x_ref[...], g_ref[...]).astype(BF16)
        for r in range(tm // CHUNK):
            rows = slice(r * CHUNK, (r + 1) * CHUNK)
            for g in range(SGU_GROUPS):
                cols = slice(g * CHUNK, (g + 1) * CHUNK)
                mixed = jnp.dot(sw_ref[g], sv_ref[rows, cols], preferred_element_type=F32)
                mixed = mixed + sbt_ref[:, g:g + 1]
                osgu_ref[rows, cols] = (su_ref[rows, cols].astype(F32) * mixed).astype(BF16)

    h = h_ref[...]
    ga = jax.nn.sigmoid(jnp.dot(h, wga_ref[...], preferred_element_type=F32))
    gs = jax.nn.sigmoid(jnp.dot(h, wgs_ref[...], preferred_element_type=F32))
    ba = jnp.dot(oa_ref[...], wba_ref[...], preferred_element_type=F32)
    bs = jnp.dot(osgu_ref[...], wbs_ref[...], preferred_element_type=F32)
    merged = (ga * ba + gs * bs).astype(BF16)
    acc_ref[...] += jnp.dot(merged, wo_ref[...], preferred_element_type=F32)

    @pl.when(j == pl.num_programs(1) - 1)
    def _():
        y_ref[...] = x_ref[...] + acc_ref[...]


def _merge(x2, z, oa, norm_g, w_in_bf, wba, wbs, wo, sgu_w_bf, sgu_bt, tm, tn):
    T = x2.shape[0]
    su_blk = 3 * ATTN_W // SGU_W
    return pl.pallas_call(
        _merge_kernel,
        grid=(T // tm, D_MODEL // tn),
        in_specs=[
            pl.BlockSpec((tm, D_MODEL), lambda i, j: (i, 0)),
            pl.BlockSpec((tm, SGU_W), lambda i, j: (i, su_blk)),
            pl.BlockSpec((tm, SGU_W), lambda i, j: (i, su_blk + 1)),
            pl.BlockSpec((tm, ATTN_W), lambda i, j: (i, 0)),
            pl.BlockSpec((1, D_MODEL), lambda i, j: (0, 0)),
            pl.BlockSpec((D_MODEL, tn), lambda i, j: (0, GATE_A_COL // tn + j)),
            pl.BlockSpec((D_MODEL, tn), lambda i, j: (0, GATE_S_COL // tn + j)),
            pl.BlockSpec((ATTN_W, tn), lambda i, j: (0, j)),
            pl.BlockSpec((SGU_W, tn), lambda i, j: (0, j)),
            pl.BlockSpec((tn, D_MODEL), lambda i, j: (j, 0)),
            pl.BlockSpec((SGU_GROUPS, CHUNK, CHUNK), lambda i, j: (0, 0, 0)),
            pl.BlockSpec((CHUNK, SGU_GROUPS), lambda i, j: (0, 0)),
        ],
        out_specs=pl.BlockSpec((tm, D_MODEL), lambda i, j: (i, 0)),
        out_shape=jax.ShapeDtypeStruct((T, D_MODEL), F32),
        scratch_shapes=[pltpu.VMEM((tm, D_MODEL), BF16),
                        pltpu.VMEM((tm, SGU_W), BF16),
                        pltpu.VMEM((tm, D_MODEL), F32)],
        compiler_params=_params(("parallel", "arbitrary")),
    )(x2, z, z, oa, norm_g, w_in_bf, w_in_bf, wba, wbs, wo, sgu_w_bf, sgu_bt)


def _mlp_kernel(x_ref, g_ref, wu_ref, wd_ref, fg_ref, y_ref, h_ref, acc_ref, *, final_norm):
    j = pl.program_id(1)

    @pl.when(j == 0)
    def _():
        acc_ref[...] = jnp.zeros(acc_ref.shape, F32)
        h_ref[...] = _rms(x_ref[...], g_ref[...]).astype(BF16)

    u = jnp.dot(h_ref[...], wu_ref[...], preferred_element_type=F32)
    u = jnp.square(jnp.maximum(u, 0.0)).astype(BF16)
    acc_ref[...] += jnp.dot(u, wd_ref[...], preferred_element_type=F32)

    @pl.when(j == pl.num_programs(1) - 1)
    def _():
        y = x_ref[...] + acc_ref[...]
        if final_norm:
            y = _rms(y, fg_ref[...])
        y_ref[...] = y


def _mlp(x2, norm_g, wu, wd, final_g, final_norm, tm, tf):
    T = x2.shape[0]
    return pl.pallas_call(
        functools.partial(_mlp_kernel, final_norm=final_norm),
        grid=(T // tm, D_FF // tf),
        in_specs=[
            pl.BlockSpec((tm, D_MODEL), lambda i, j: (i, 0)),
            pl.BlockSpec((1, D_MODEL), lambda i, j: (0, 0)),
            pl.BlockSpec((D_MODEL, tf), lambda i, j: (0, j)),
            pl.BlockSpec((tf, D_MODEL), lambda i, j: (j, 0)),
            pl.BlockSpec((1, D_MODEL), lambda i, j: (0, 0)),
        ],
        out_specs=pl.BlockSpec((tm, D_MODEL), lambda i, j: (i, 0)),
        out_shape=jax.ShapeDtypeStruct((T, D_MODEL), F32),
        scratch_shapes=[pltpu.VMEM((tm, D_MODEL), BF16),
                        pltpu.VMEM((tm, D_MODEL), F32)],
        compiler_params=_params(("parallel", "arbitrary")),
    )(x2, norm_g, wu, wd, final_g)


def _rotary_lane_order():
    d = np.arange(DH)
    rest = d[ROT_DIM:]
    split = DH // 2 - ROT_HALF
    return np.concatenate([d[:ROT_HALF], rest[:split], d[ROT_HALF:ROT_DIM], rest[split:]])


def _in_proj_columns():
    comp = _rotary_lane_order()
    qk = (np.arange(2 * ATTN_W // DH)[:, None] * DH + comp[None, :]).reshape(-1)
    return np.concatenate([qk, np.arange(2 * ATTN_W, IN_COLS)])


def _rot_tables(seq_len):
    pos = jnp.arange(seq_len, dtype=F32)
    inv_freq = 1.0 / (jnp.float32(ROPE_THETA) ** (jnp.arange(0, ROT_DIM, 2, dtype=F32) / ROT_DIM))
    ang = pos[:, None] * inv_freq[None, :]
    cos, sin = jnp.cos(ang), jnp.sin(ang)
    pad = DH // 2 - ROT_HALF
    ones, zeros = jnp.ones((seq_len, pad), F32), jnp.zeros((seq_len, pad), F32)
    rc = jnp.concatenate([cos, ones, cos, ones], axis=-1)
    rs = jnp.concatenate([-sin, zeros, sin, zeros], axis=-1)
    return jnp.stack([rc * EXP2_SCALE, rc]), jnp.stack([rs * EXP2_SCALE, rs])


def _tiles(seq_len):
    tm = min(512, seq_len)
    tq = min(512, seq_len)
    tk = min(512, seq_len // 2)
    return tm, tq, tk


def _prepare(p):
    row = lambda a: a.reshape(1, -1).astype(F32)
    cols = _in_proj_columns()
    layers = []
    for l in range(p["w_in"].shape[0]):
        layers.append(dict(
            attn_norm_g=row(p["attn_norm_g"][l]), w_in=p["w_in"][l].astype(BF16)[:, cols],
            lq1=row(p["lambda_q1"][l]), lk1=row(p["lambda_k1"][l]),
            lq2=row(p["lambda_q2"][l]), lk2=row(p["lambda_k2"][l]),
            subln_g=row(p["subln_g"][l]), ln_g=row(p["sgu_ln_g"][l]), ln_b=row(p["sgu_ln_b"][l]),
            sgu_w=p["sgu_w"][l].astype(BF16), sgu_bt=p["sgu_b"][l].T.astype(F32),
            w_br_attn=p["w_br_attn"][l].astype(BF16), w_br_sgu=p["w_br_sgu"][l].astype(BF16),
            w_out=p["w_out"][l].astype(BF16), mlp_norm_g=row(p["mlp_norm_g"][l]),
            w_up=p["w_up"][l].astype(BF16), w_down=p["w_down"][l].astype(BF16)))
    return layers, row(p["final_norm_g"])


def _trunk(x, layers, final_g):
    B, S, _ = x.shape
    tm, tq, tk = _tiles(S)
    rot = _rot_tables(S)
    x2 = x.reshape(B * S, D_MODEL)
    for l, w in enumerate(layers):
        z = _in_proj(x2, S, w["attn_norm_g"], w["w_in"], rot, w["ln_g"], w["ln_b"], tm)
        oa = _attention(z.reshape(B, S, ZCOLS), w["lq1"], w["lk1"], w["lq2"], w["lk2"],
                        w["subln_g"], _lambda_init(l), tq, tk)
        x2 = _merge(x2, z, oa.reshape(B * S, ATTN_W), w["attn_norm_g"], w["w_in"],
                    w["w_br_attn"], w["w_br_sgu"], w["w_out"], w["sgu_w"], w["sgu_bt"],
                    tm, MERGE_TN)
        x2 = _mlp(x2, w["mlp_norm_g"], w["w_up"], w["w_down"], final_g,
                  l == len(layers) - 1, tm, MLP_TF)
    return x2.reshape(B, S, D_MODEL)


def kernel(x_prompt, x_sample, attn_norm_g, w_in, lambda_q1, lambda_k1, lambda_q2, lambda_k2,
           subln_g, sgu_ln_g, sgu_ln_b, sgu_w, sgu_b, w_br_attn, w_br_sgu, w_out, mlp_norm_g,
           w_up, w_down, final_norm_g):
    p = dict(attn_norm_g=attn_norm_g, w_in=w_in, lambda_q1=lambda_q1, lambda_k1=lambda_k1,
             lambda_q2=lambda_q2, lambda_k2=lambda_k2, subln_g=subln_g, sgu_ln_g=sgu_ln_g,
             sgu_ln_b=sgu_ln_b, sgu_w=sgu_w, sgu_b=sgu_b, w_br_attn=w_br_attn, w_br_sgu=w_br_sgu,
             w_out=w_out, mlp_norm_g=mlp_norm_g, w_up=w_up, w_down=w_down,
             final_norm_g=final_norm_g)
    layers, final_g = _prepare(p)
    return _trunk(x_prompt, layers, final_g), _trunk(x_sample, layers, final_g)
```

```python
import functools
import math

import jax
import jax.numpy as jnp
from jax import lax
from jax.experimental import pallas as pl
from jax.experimental.pallas import tpu as pltpu

D_MODEL = 2048
N_HEADS = 4
DH = 128
HEAD_W = 2 * DH
ATTN_W = N_HEADS * HEAD_W
ROT_DIM = DH // 4
ROT_HALF = ROT_DIM // 2
ROPE_THETA = 500000.0
ATTN_SCALE = DH ** -0.5
CHUNK = 128
SGU_GROUPS = 8
SGU_W = SGU_GROUPS * CHUNK
D_FF = 4 * D_MODEL
PROJ_COLS = 3 * ATTN_W + 2 * SGU_W
ZCOLS = 3 * ATTN_W + SGU_W
RMS_EPS = 1e-6
LN_EPS = 1e-5
SQRT_HALF = math.sqrt(0.5)
EXP2_SCALE = ATTN_SCALE * math.log2(math.e)

VMEM_LIMIT_BYTES_V7X = 56 * 1024 * 1024

MERGE_TN = 512
MLP_TF = 1024

F32 = jnp.float32
BF16 = jnp.bfloat16


def _lambda_init(layer_idx):
    return 0.8 - 0.6 * math.exp(-0.3 * layer_idx)


def _params(semantics):
    return pltpu.CompilerParams(dimension_semantics=semantics,
                                vmem_limit_bytes=VMEM_LIMIT_BYTES_V7X)


def _rms(x, g):
    ms = jnp.mean(x * x, axis=-1, keepdims=True)
    return x * lax.rsqrt(ms + RMS_EPS) * g


def _gelu(x):
    return 0.5 * x * (1.0 + lax.erf(x * SQRT_HALF))


def _in_proj_kernel(x_ref, g_ref, w_ref, rcq_ref, rsq_ref, rck_ref, rsk_ref, lng_ref, lnb_ref,
                    sw_ref, sbt_ref, z_ref, h_ref, su_ref, svn_ref):
    tm = x_ref.shape[0]
    h_ref[...] = _rms(x_ref[...], g_ref[...]).astype(BF16)

    def proj(c):
        return jnp.dot(h_ref[...], w_ref[:, c * ATTN_W:(c + 1) * ATTN_W],
                       preferred_element_type=F32)

    for c, (rc_ref, rs_ref) in enumerate(((rcq_ref, rsq_ref), (rck_ref, rsk_ref))):
        acc = proj(c)
        rc, rs = rc_ref[...], rs_ref[...]
        for u in range(ATTN_W // DH):
            t = acc[:, u * DH:(u + 1) * DH]
            lanes = slice(c * ATTN_W + u * DH, c * ATTN_W + (u + 1) * DH)
            z_ref[:, lanes] = (t * rc + pltpu.roll(t, DH // 2, 1) * rs).astype(BF16)

    z_ref[:, 2 * ATTN_W:3 * ATTN_W] = proj(2).astype(BF16)

    su_ref[...] = _gelu(proj(3))
    y = _gelu(proj(4))
    mu = jnp.mean(y, axis=-1, keepdims=True)
    yc = y - mu
    var = jnp.mean(yc * yc, axis=-1, keepdims=True)
    svn_ref[...] = (yc * lax.rsqrt(var + LN_EPS) * lng_ref[...] + lnb_ref[...]).astype(BF16)

    for r in range(tm // CHUNK):
        rows = slice(r * CHUNK, (r + 1) * CHUNK)
        for g in range(SGU_GROUPS):
            cols = slice(g * CHUNK, (g + 1) * CHUNK)
            mixed = jnp.dot(sw_ref[g], svn_ref[rows, cols], preferred_element_type=F32)
            mixed = mixed + sbt_ref[:, g:g + 1]
            out_cols = slice(3 * ATTN_W + g * CHUNK, 3 * ATTN_W + (g + 1) * CHUNK)
            z_ref[rows, out_cols] = (su_ref[rows, cols] * mixed).astype(BF16)


def _in_proj(x2, seq_len, norm_g, w_proj, rot, ln_g, ln_b, sgu_w_bf, sgu_bt, tm):
    T = x2.shape[0]
    n_pos_blocks = seq_len // tm
    rot_spec = pl.BlockSpec((tm, DH), lambda i: (i % n_pos_blocks, 0))
    whole = lambda a: pl.BlockSpec(a.shape, lambda i: (0,) * a.ndim)
    return pl.pallas_call(
        _in_proj_kernel,
        grid=(T // tm,),
        in_specs=[
            pl.BlockSpec((tm, D_MODEL), lambda i: (i, 0)),
            whole(norm_g), whole(w_proj),
            rot_spec, rot_spec, rot_spec, rot_spec,
            whole(ln_g), whole(ln_b), whole(sgu_w_bf), whole(sgu_bt),
        ],
        out_specs=pl.BlockSpec((tm, ZCOLS), lambda i: (i, 0)),
        out_shape=jax.ShapeDtypeStruct((T, ZCOLS), BF16),
        scratch_shapes=[pltpu.VMEM((tm, D_MODEL), BF16),
                        pltpu.VMEM((tm, SGU_W), F32),
                        pltpu.VMEM((tm, SGU_W), BF16)],
        compiler_params=_params(("parallel",)),
    )(x2, norm_g, w_proj, rot[0], rot[1], rot[2], rot[3], ln_g, ln_b, sgu_w_bf, sgu_bt)


def _attn_kernel(q_ref, k_ref, v_ref, lq1_ref, lk1_ref, lq2_ref, lk2_ref, g_ref, o_ref,
                 s0_ref, s1_ref, p0_ref, p1_ref, acc_ref, *, tk, n_kv, lambda_init):
    s_refs, p_refs = (s0_ref, s1_ref), (p0_ref, p1_ref)
    q = q_ref[...]

    def chunk_rows(j):
        return pl.ds(pl.multiple_of(j * tk, tk), tk)

    def scores(j, slot):
        rows = chunk_rows(j)
        col_max = []
        for c in range(2):
            s = lax.dot_general(k_ref[rows, c * DH:(c + 1) * DH], q[:, c * DH:(c + 1) * DH],
                                (((1,), (1,)), ((), ())), preferred_element_type=F32)
            s_refs[slot][c] = s
            col_max.append(jnp.max(s, axis=0, keepdims=True))
        return tuple(col_max)

    def softmax(slot, col_max, stats):
        new_stats, alphas = [], []
        for c in range(2):
            m, l = stats[c]
            m_new = jnp.maximum(m, col_max[c])
            alpha = jnp.exp2(m - m_new)
            p = jnp.exp2(s_refs[slot][c] - m_new)
            p_refs[slot][c] = p.astype(BF16)
            new_stats.append((m_new, alpha * l + jnp.sum(p, axis=0, keepdims=True)))
            alphas.append(alpha)
        return tuple(new_stats), tuple(alphas)

    def values(j, slot, alphas):
        v = v_ref[chunk_rows(j), :]
        for c in range(2):
            pv = lax.dot_general(v, p_refs[slot][c], (((0,), (0,)), ((), ())),
                                 preferred_element_type=F32)
            acc_ref[c] = alphas[c] * acc_ref[c] + pv

    tq = q_ref.shape[0]
    acc_ref[...] = jnp.zeros(acc_ref.shape, F32)
    stats = tuple((jnp.full((1, tq), -jnp.inf, F32), jnp.zeros((1, tq), F32)) for _ in range(2))

    col_max = scores(0, 0)
    col_max_next = scores(1, 1)
    stats, alphas = softmax(0, col_max, stats)

    def two_chunks(u, carry):
        col_max, stats, alphas = carry
        col_max_a = scores(2 * u + 2, 0)
        stats, alphas_a = softmax(1, col_max, stats)
        values(2 * u, 0, alphas)
        col_max_b = scores(2 * u + 3, 1)
        stats, alphas_b = softmax(0, col_max_a, stats)
        values(2 * u + 1, 1, alphas_a)
        return col_max_b, stats, alphas_b

    col_max, stats, alphas = lax.fori_loop(0, n_kv // 2 - 1, two_chunks,
                                           (col_max_next, stats, alphas))
    stats, alphas_last = softmax(1, col_max, stats)
    values(n_kv - 2, 0, alphas)
    values(n_kv - 1, 1, alphas_last)

    (_, l1), (_, l2) = stats
    lam = (jnp.exp(jnp.sum(lq1_ref[...] * lk1_ref[...], axis=-1, keepdims=True))
           - jnp.exp(jnp.sum(lq2_ref[...] * lk2_ref[...], axis=-1, keepdims=True))
           + lambda_init)
    o_t = acc_ref[0] * (1.0 / l1) - lam * (acc_ref[1] * (1.0 / l2))
    ms = jnp.mean(o_t * o_t, axis=0, keepdims=True)
    y = (o_t * lax.rsqrt(ms + RMS_EPS)).T
    o_ref[...] = (y * g_ref[...] * (1.0 - lambda_init)).astype(BF16)


def _attention(z3, lq1, lk1, lq2, lk2, subln_g, lambda_init, tq, tk):
    B, S, _ = z3.shape
    n_kv = S // tk
    assert n_kv >= 2 and n_kv % 2 == 0, "the chunk pipeline takes an even number of key chunks"
    vec = lambda n: pl.BlockSpec((1, n), lambda b, h, i: (0, 0))
    kern = functools.partial(_attn_kernel, tk=tk, n_kv=n_kv, lambda_init=lambda_init)
    k_blk0 = ATTN_W // HEAD_W
    v_blk0 = 2 * ATTN_W // HEAD_W
    slot = lambda dt: pltpu.VMEM((2, tk, tq), dt)
    return pl.pallas_call(
        kern,
        grid=(B, N_HEADS, S // tq),
        in_specs=[
            pl.BlockSpec((None, tq, HEAD_W), lambda b, h, i: (b, i, h)),
            pl.BlockSpec((None, S, HEAD_W), lambda b, h, i: (b, 0, k_blk0 + h)),
            pl.BlockSpec((None, S, HEAD_W), lambda b, h, i: (b, 0, v_blk0 + h)),
            vec(DH), vec(DH), vec(DH), vec(DH),
            vec(HEAD_W),
        ],
        out_specs=pl.BlockSpec((None, tq, HEAD_W), lambda b, h, i: (b, i, h)),
        out_shape=jax.ShapeDtypeStruct((B, S, ATTN_W), BF16),
        scratch_shapes=[slot(F32), slot(F32), slot(BF16), slot(BF16),
                        pltpu.VMEM((2, HEAD_W, tq), F32)],
        compiler_params=_params(("parallel", "parallel", "arbitrary")),
    )(z3, z3, z3, lq1, lk1, lq2, lk2, subln_g)


def _merge_kernel(x_ref, osgu_ref, oa_ref, g_ref, wga_ref, wgs_ref, wba_ref, wbs_ref, wo_ref,
                  y_ref, h_ref):
    @pl.when(pl.program_id(1) == 0)
    def _():
        x = x_ref[...]
        y_ref[...] = x
        h_ref[...] = _rms(x, g_ref[...]).astype(BF16)

    h = h_ref[...]
    ga = jax.nn.sigmoid(jnp.dot(h, wga_ref[...], preferred_element_type=F32))
    gs = jax.nn.sigmoid(jnp.dot(h, wgs_ref[...], preferred_element_type=F32))
    ba = jnp.dot(oa_ref[...], wba_ref[...], preferred_element_type=F32)
    bs = jnp.dot(osgu_ref[...], wbs_ref[...], preferred_element_type=F32)
    merged = (ga * ba + gs * bs).astype(BF16)
    y_ref[...] += jnp.dot(merged, wo_ref[...], preferred_element_type=F32)


def _merge(x2, z, oa, norm_g, w_gates, wba, wbs, wo, tm, tn):
    T = x2.shape[0]
    return pl.pallas_call(
        _merge_kernel,
        grid=(T // tm, D_MODEL // tn),
        in_specs=[
            pl.BlockSpec((tm, D_MODEL), lambda i, j: (i, 0)),
            pl.BlockSpec((tm, SGU_W), lambda i, j: (i, 3 * ATTN_W // SGU_W)),
            pl.BlockSpec((tm, ATTN_W), lambda i, j: (i, 0)),
            pl.BlockSpec((1, D_MODEL), lambda i, j: (0, 0)),
            pl.BlockSpec((D_MODEL, tn), lambda i, j: (0, j)),
            pl.BlockSpec((D_MODEL, tn), lambda i, j: (0, D_MODEL // tn + j)),
            pl.BlockSpec((ATTN_W, tn), lambda i, j: (0, j)),
            pl.BlockSpec((SGU_W, tn), lambda i, j: (0, j)),
            pl.BlockSpec((tn, D_MODEL), lambda i, j: (j, 0)),
        ],
        out_specs=pl.BlockSpec((tm, D_MODEL), lambda i, j: (i, 0)),
        out_shape=jax.ShapeDtypeStruct((T, D_MODEL), F32),
        scratch_shapes=[pltpu.VMEM((tm, D_MODEL), BF16)],
        compiler_params=_params(("parallel", "arbitrary")),
    )(x2, z, oa, norm_g, w_gates, w_gates, wba, wbs, wo)


def _mlp_kernel(x_ref, g_ref, wu_ref, wd_ref, fg_ref, y_ref, h_ref, *, final_norm):
    j = pl.program_id(1)

    @pl.when(j == 0)
    def _():
        x = x_ref[...]
        y_ref[...] = x
        h_ref[...] = _rms(x, g_ref[...]).astype(BF16)

    u = jnp.dot(h_ref[...], wu_ref[...], preferred_element_type=F32)
    u = jnp.square(jnp.maximum(u, 0.0)).astype(BF16)
    y_ref[...] += jnp.dot(u, wd_ref[...], preferred_element_type=F32)

    if final_norm:
        @pl.when(j == pl.num_programs(1) - 1)
        def _():
            y_ref[...] = _rms(y_ref[...], fg_ref[...])


def _mlp(x2, norm_g, wu, wd, final_g, final_norm, tm, tf):
    T = x2.shape[0]
    return pl.pallas_call(
        functools.partial(_mlp_kernel, final_norm=final_norm),
        grid=(T // tm, D_FF // tf),
        in_specs=[
            pl.BlockSpec((tm, D_MODEL), lambda i, j: (i, 0)),
            pl.BlockSpec((1, D_MODEL), lambda i, j: (0, 0)),
            pl.BlockSpec((D_MODEL, tf), lambda i, j: (0, j)),
            pl.BlockSpec((tf, D_MODEL), lambda i, j: (j, 0)),
            pl.BlockSpec((1, D_MODEL), lambda i, j: (0, 0)),
        ],
        out_specs=pl.BlockSpec((tm, D_MODEL), lambda i, j: (i, 0)),
        out_shape=jax.ShapeDtypeStruct((T, D_MODEL), F32),
        scratch_shapes=[pltpu.VMEM((tm, D_MODEL), BF16)],
        compiler_params=_params(("parallel", "arbitrary")),
    )(x2, norm_g, wu, wd, final_g)


def _rotary_lane_order(w):
    split = ROT_DIM + DH // 2 - ROT_HALF
    return jnp.concatenate([w[..., :ROT_HALF], w[..., ROT_DIM:split],
                            w[..., ROT_HALF:ROT_DIM], w[..., split:]], axis=-1)


def _rot_tables(seq_len):
    pos = jnp.arange(seq_len, dtype=F32)
    inv_freq = 1.0 / (jnp.float32(ROPE_THETA) ** (jnp.arange(0, ROT_DIM, 2, dtype=F32) / ROT_DIM))
    ang = pos[:, None] * inv_freq[None, :]
    cos, sin = jnp.cos(ang), jnp.sin(ang)
    pad = DH // 2 - ROT_HALF
    ones, zeros = jnp.ones((seq_len, pad), F32), jnp.zeros((seq_len, pad), F32)
    rc = jnp.concatenate([cos, ones, cos, ones], axis=-1)
    rs = jnp.concatenate([-sin, zeros, sin, zeros], axis=-1)
    return rc * EXP2_SCALE, rs * EXP2_SCALE, rc, rs


def _tiles(seq_len):
    tm = min(512, seq_len)
    tq = min(512, seq_len)
    tk = min(512, seq_len // 2)
    return tm, tq, tk


def _prepare(p):
    row = lambda a: a.reshape(1, -1).astype(F32)
    layers = []
    for l in range(p["w_in"].shape[0]):
        w_in = p["w_in"][l]
        w_qk = _rotary_lane_order(w_in[:, :2 * ATTN_W].reshape(D_MODEL, -1, DH))
        w_proj = jnp.concatenate([w_qk.reshape(D_MODEL, 2 * ATTN_W),
                                  w_in[:, 2 * ATTN_W:PROJ_COLS]], axis=1).astype(BF16)
        layers.append(dict(
            attn_norm_g=row(p["attn_norm_g"][l]), w_proj=w_proj,
            w_gates=w_in[:, PROJ_COLS:].astype(BF16),
            lq1=row(p["lambda_q1"][l]), lk1=row(p["lambda_k1"][l]),
            lq2=row(p["lambda_q2"][l]), lk2=row(p["lambda_k2"][l]),
            subln_g=row(p["subln_g"][l]), ln_g=row(p["sgu_ln_g"][l]), ln_b=row(p["sgu_ln_b"][l]),
            sgu_w=p["sgu_w"][l].astype(BF16), sgu_bt=p["sgu_b"][l].T.astype(F32),
            w_br_attn=p["w_br_attn"][l].astype(BF16), w_br_sgu=p["w_br_sgu"][l].astype(BF16),
            w_out=p["w_out"][l].astype(BF16), mlp_norm_g=row(p["mlp_norm_g"][l]),
            w_up=p["w_up"][l].astype(BF16), w_down=p["w_down"][l].astype(BF16)))
    return layers, row(p["final_norm_g"])


def _trunk(x, layers, final_g):
    B, S, _ = x.shape
    tm, tq, tk = _tiles(S)
    rot = _rot_tables(S)
    x2 = x.reshape(B * S, D_MODEL)
    for l, w in enumerate(layers):
        z = _in_proj(x2, S, w["attn_norm_g"], w["w_proj"], rot, w["ln_g"], w["ln_b"],
                     w["sgu_w"], w["sgu_bt"], tm)
        oa = _attention(z.reshape(B, S, ZCOLS), w["lq1"], w["lk1"], w["lq2"], w["lk2"],
                        w["subln_g"], _lambda_init(l), tq, tk)
        x2 = _merge(x2, z, oa.reshape(B * S, ATTN_W), w["attn_norm_g"], w["w_gates"],
                    w["w_br_attn"], w["w_br_sgu"], w["w_out"], tm, MERGE_TN)
        x2 = _mlp(x2, w["mlp_norm_g"], w["w_up"], w["w_down"], final_g,
                  l == len(layers) - 1, tm, MLP_TF)
    return x2.reshape(B, S, D_MODEL)


def kernel(x_prompt, x_sample, attn_norm_g, w_in, lambda_q1, lambda_k1, lambda_q2, lambda_k2,
           subln_g, sgu_ln_g, sgu_ln_b, sgu_w, sgu_b, w_br_attn, w_br_sgu, w_out, mlp_norm_g,
           w_up, w_down, final_norm_g):
    p = dict(attn_norm_g=attn_norm_g, w_in=w_in, lambda_q1=lambda_q1, lambda_k1=lambda_k1,
             lambda_q2=lambda_q2, lambda_k2=lambda_k2, subln_g=subln_g, sgu_ln_g=sgu_ln_g,
             sgu_ln_b=sgu_ln_b, sgu_w=sgu_w, sgu_b=sgu_b, w_br_attn=w_br_attn, w_br_sgu=w_br_sgu,
             w_out=w_out, mlp_norm_g=mlp_norm_g, w_up=w_up, w_down=w_down,
             final_norm_g=final_norm_g)
    layers, final_g = _prepare(p)
    return _trunk(x_prompt, layers, final_g), _trunk(x_sample, layers, final_g)
```

```python
import functools
import math

import jax
import jax.numpy as jnp
from jax import lax
from jax.experimental import pallas as pl
from jax.experimental.pallas import tpu as pltpu

D_MODEL = 2048
N_HEADS = 4
DH = 128
HEAD_W = 2 * DH
ATTN_W = N_HEADS * HEAD_W
ONES_ROWS = 16
VT_ROWS = HEAD_W + ONES_ROWS
ROT_DIM = DH // 4
ROT_HALF = ROT_DIM // 2
ROPE_THETA = 500000.0
ATTN_SCALE = DH ** -0.5
CHUNK = 128
SGU_GROUPS = 8
SGU_W = SGU_GROUPS * CHUNK
D_FF = 4 * D_MODEL
PROJ_COLS = 3 * ATTN_W + 2 * SGU_W
ZCOLS = ATTN_W + SGU_W
RMS_EPS = 1e-6
LN_EPS = 1e-5
SQRT_HALF = math.sqrt(0.5)
EXP2_SCALE = ATTN_SCALE * math.log2(math.e)

VMEM_LIMIT_BYTES_V7X = 56 * 1024 * 1024

MERGE_TN = 512
MLP_TF = 1024

F32 = jnp.float32
BF16 = jnp.bfloat16


def _lambda_init(layer_idx):
    return 0.8 - 0.6 * math.exp(-0.3 * layer_idx)


def _params(semantics):
    return pltpu.CompilerParams(dimension_semantics=semantics,
                                vmem_limit_bytes=VMEM_LIMIT_BYTES_V7X)


def _rms(x, g):
    ms = jnp.mean(x * x, axis=-1, keepdims=True)
    return x * lax.rsqrt(ms + RMS_EPS) * g


def _gelu(x):
    return 0.5 * x * (1.0 + lax.erf(x * SQRT_HALF))


def _in_proj_kernel(x_ref, g_ref, w_ref, rcq_ref, rsq_ref, rck_ref, rsk_ref, lng_ref, lnb_ref,
                    sw_ref, sbt_ref, qt_ref, vt_ref, z_ref, h_ref, su_ref, svn_ref):
    tm = x_ref.shape[0]
    h_ref[...] = _rms(x_ref[...], g_ref[...]).astype(BF16)

    def proj(c):
        return jnp.dot(h_ref[...], w_ref[:, c * ATTN_W:(c + 1) * ATTN_W],
                       preferred_element_type=F32)

    def rotary(acc, rc, rs, u):
        t = acc[:, u * DH:(u + 1) * DH]
        return t * rc + pltpu.roll(t, DH // 2, 1) * rs

    acc = proj(0)
    rc, rs = rcq_ref[...], rsq_ref[...]
    for u in range(ATTN_W // DH):
        qt_ref[u * DH:(u + 1) * DH, :] = rotary(acc, rc, rs, u).T.astype(BF16)

    acc = proj(1)
    rc, rs = rck_ref[...], rsk_ref[...]
    for u in range(ATTN_W // DH):
        z_ref[:, u * DH:(u + 1) * DH] = rotary(acc, rc, rs, u).astype(BF16)

    vt = proj(2).T.astype(BF16)
    for hd in range(N_HEADS):
        vt_ref[hd * VT_ROWS:hd * VT_ROWS + HEAD_W, :] = vt[hd * HEAD_W:(hd + 1) * HEAD_W, :]
        vt_ref[hd * VT_ROWS + HEAD_W:(hd + 1) * VT_ROWS, :] = jnp.ones((ONES_ROWS, tm), BF16)

    su_ref[...] = _gelu(proj(3))
    y = _gelu(proj(4))
    mu = jnp.mean(y, axis=-1, keepdims=True)
    yc = y - mu
    var = jnp.mean(yc * yc, axis=-1, keepdims=True)
    svn_ref[...] = (yc * lax.rsqrt(var + LN_EPS) * lng_ref[...] + lnb_ref[...]).astype(BF16)

    for r in range(tm // CHUNK):
        rows = slice(r * CHUNK, (r + 1) * CHUNK)
        for g in range(SGU_GROUPS):
            cols = slice(g * CHUNK, (g + 1) * CHUNK)
            mixed = jnp.dot(sw_ref[g], svn_ref[rows, cols], preferred_element_type=F32)
            mixed = mixed + sbt_ref[:, g:g + 1]
            out_cols = slice(ATTN_W + g * CHUNK, ATTN_W + (g + 1) * CHUNK)
            z_ref[rows, out_cols] = (su_ref[rows, cols] * mixed).astype(BF16)


def _in_proj(x2, batch, seq_len, norm_g, w_proj, rot, ln_g, ln_b, sgu_w_bf, sgu_bt, tm):
    T = x2.shape[0]
    n_pos_blocks = seq_len // tm
    rot_spec = pl.BlockSpec((tm, DH), lambda i: (i % n_pos_blocks, 0))
    whole = lambda a: pl.BlockSpec(a.shape, lambda i: (0,) * a.ndim)
    t_spec = lambda rows: pl.BlockSpec((None, None, rows, tm),
                                       lambda i: (i // n_pos_blocks, i % n_pos_blocks, 0, 0))
    t_shape = lambda rows: jax.ShapeDtypeStruct((batch, n_pos_blocks, rows, tm), BF16)
    return pl.pallas_call(
        _in_proj_kernel,
        grid=(T // tm,),
        in_specs=[
            pl.BlockSpec((tm, D_MODEL), lambda i: (i, 0)),
            whole(norm_g), whole(w_proj),
            rot_spec, rot_spec, rot_spec, rot_spec,
            whole(ln_g), whole(ln_b), whole(sgu_w_bf), whole(sgu_bt),
        ],
        out_specs=[t_spec(ATTN_W), t_spec(N_HEADS * VT_ROWS),
                   pl.BlockSpec((tm, ZCOLS), lambda i: (i, 0))],
        out_shape=[t_shape(ATTN_W), t_shape(N_HEADS * VT_ROWS),
                   jax.ShapeDtypeStruct((T, ZCOLS), BF16)],
        scratch_shapes=[pltpu.VMEM((tm, D_MODEL), BF16),
                        pltpu.VMEM((tm, SGU_W), F32),
                        pltpu.VMEM((tm, SGU_W), BF16)],
        compiler_params=_params(("parallel",)),
    )(x2, norm_g, w_proj, rot[0], rot[1], rot[2], rot[3], ln_g, ln_b, sgu_w_bf, sgu_bt)


def _attn_kernel(qt_ref, k_ref, vt_ref, lq1_ref, lk1_ref, lq2_ref, lk2_ref, g_ref, o_ref,
                 s0_ref, s1_ref, p0_ref, p1_ref, acc_ref, *, tk, n_kv, lambda_init):
    s_refs, p_refs = (s0_ref, s1_ref), (p0_ref, p1_ref)

    def scores(j, slot):
        rows = pl.ds(pl.multiple_of(j * tk, tk), tk)
        col_max = []
        for c in range(2):
            s = jnp.dot(k_ref[rows, c * DH:(c + 1) * DH], qt_ref[c * DH:(c + 1) * DH, :],
                        preferred_element_type=F32)
            s_refs[slot][c] = s
            col_max.append(jnp.max(s, axis=0, keepdims=True))
        return tuple(col_max)

    def softmax(slot, col_max, run_max):
        new_max, alphas = [], []
        for c in range(2):
            m_new = jnp.maximum(run_max[c], col_max[c])
            alphas.append(jnp.exp2(run_max[c] - m_new))
            p_refs[slot][c] = jnp.exp2(s_refs[slot][c] - m_new).astype(BF16)
            new_max.append(m_new)
        return tuple(new_max), tuple(alphas)

    def values(j, slot, alphas):
        vt = vt_ref[j]
        for c in range(2):
            pv = jnp.dot(vt, p_refs[slot][c], preferred_element_type=F32)
            acc_ref[c] = alphas[c] * acc_ref[c] + pv

    tq = qt_ref.shape[1]
    acc_ref[...] = jnp.zeros(acc_ref.shape, F32)
    run_max = tuple(jnp.full((1, tq), -jnp.inf, F32) for _ in range(2))

    col_max = scores(0, 0)
    col_max_next = scores(1, 1)
    run_max, alphas = softmax(0, col_max, run_max)

    def two_chunks(u, carry):
        col_max, run_max, alphas = carry
        col_max_a = scores(2 * u + 2, 0)
        run_max, alphas_a = softmax(1, col_max, run_max)
        values(2 * u, 0, alphas)
        col_max_b = scores(2 * u + 3, 1)
        run_max, alphas_b = softmax(0, col_max_a, run_max)
        values(2 * u + 1, 1, alphas_a)
        return col_max_b, run_max, alphas_b

    col_max, run_max, alphas = lax.fori_loop(0, n_kv // 2 - 1, two_chunks,
                                             (col_max_next, run_max, alphas))
    _, alphas_last = softmax(1, col_max, run_max)
    values(n_kv - 2, 0, alphas)
    values(n_kv - 1, 1, alphas_last)

    l1, l2 = acc_ref[0, HEAD_W:HEAD_W + 1, :], acc_ref[1, HEAD_W:HEAD_W + 1, :]
    lam = (jnp.exp(jnp.sum(lq1_ref[...] * lk1_ref[...], axis=-1, keepdims=True))
           - jnp.exp(jnp.sum(lq2_ref[...] * lk2_ref[...], axis=-1, keepdims=True))
           + lambda_init)
    o_t = (acc_ref[0, :HEAD_W, :] * (1.0 / l1)
           - lam * (acc_ref[1, :HEAD_W, :] * (1.0 / l2)))
    ms = jnp.mean(o_t * o_t, axis=0, keepdims=True)
    y = (o_t * lax.rsqrt(ms + RMS_EPS)).T
    o_ref[...] = (y * g_ref[...] * (1.0 - lambda_init)).astype(BF16)


def _attention(qt4, z3, vt4, lq1, lk1, lq2, lk2, subln_g, lambda_init):
    B, n_kv, _, tk = vt4.shape
    n_q, tq = qt4.shape[1], qt4.shape[3]
    S = z3.shape[1]
    assert n_kv >= 2 and n_kv % 2 == 0, "the chunk pipeline takes an even number of key chunks"
    vec = lambda n: pl.BlockSpec((1, n), lambda b, h, i: (0, 0))
    kern = functools.partial(_attn_kernel, tk=tk, n_kv=n_kv, lambda_init=lambda_init)
    slot = lambda dt: pltpu.VMEM((2, tk, tq), dt)
    return pl.pallas_call(
        kern,
        grid=(B, N_HEADS, n_q),
        in_specs=[
            pl.BlockSpec((None, None, HEAD_W, tq), lambda b, h, i: (b, i, h, 0)),
            pl.BlockSpec((None, S, HEAD_W), lambda b, h, i: (b, 0, h)),
            pl.BlockSpec((None, n_kv, VT_ROWS, tk), lambda b, h, i: (b, 0, h, 0)),
            vec(DH), vec(DH), vec(DH), vec(DH),
            vec(HEAD_W),
        ],
        out_specs=pl.BlockSpec((None, tq, HEAD_W), lambda b, h, i: (b, i, h)),
        out_shape=jax.ShapeDtypeStruct((B, S, ATTN_W), BF16),
        scratch_shapes=[slot(F32), slot(F32), slot(BF16), slot(BF16),
                        pltpu.VMEM((2, VT_ROWS, tq), F32)],
        compiler_params=_params(("parallel", "parallel", "arbitrary")),
    )(qt4, z3, vt4, lq1, lk1, lq2, lk2, subln_g)


def _merge_kernel(x_ref, osgu_ref, oa_ref, g_ref, wga_ref, wgs_ref, wba_ref, wbs_ref, wo_ref,
                  y_ref, h_ref):
    @pl.when(pl.program_id(1) == 0)
    def _():
        x = x_ref[...]
        y_ref[...] = x
        h_ref[...] = _rms(x, g_ref[...]).astype(BF16)

    h = h_ref[...]
    ga = jax.nn.sigmoid(jnp.dot(h, wga_ref[...], preferred_element_type=F32))
    gs = jax.nn.sigmoid(jnp.dot(h, wgs_ref[...], preferred_element_type=F32))
    ba = jnp.dot(oa_ref[...], wba_ref[...], preferred_element_type=F32)
    bs = jnp.dot(osgu_ref[...], wbs_ref[...], preferred_element_type=F32)
    merged = (ga * ba + gs * bs).astype(BF16)
    y_ref[...] += jnp.dot(merged, wo_ref[...], preferred_element_type=F32)


def _merge(x2, z, oa, norm_g, w_gates, wba, wbs, wo, tm, tn):
    T = x2.shape[0]
    return pl.pallas_call(
        _merge_kernel,
        grid=(T // tm, D_MODEL // tn),
        in_specs=[
            pl.BlockSpec((tm, D_MODEL), lambda i, j: (i, 0)),
            pl.BlockSpec((tm, SGU_W), lambda i, j: (i, ATTN_W // SGU_W)),
            pl.BlockSpec((tm, ATTN_W), lambda i, j: (i, 0)),
            pl.BlockSpec((1, D_MODEL), lambda i, j: (0, 0)),
            pl.BlockSpec((D_MODEL, tn), lambda i, j: (0, j)),
            pl.BlockSpec((D_MODEL, tn), lambda i, j: (0, D_MODEL // tn + j)),
            pl.BlockSpec((ATTN_W, tn), lambda i, j: (0, j)),
            pl.BlockSpec((SGU_W, tn), lambda i, j: (0, j)),
            pl.BlockSpec((tn, D_MODEL), lambda i, j: (j, 0)),
        ],
        out_specs=pl.BlockSpec((tm, D_MODEL), lambda i, j: (i, 0)),
        out_shape=jax.ShapeDtypeStruct((T, D_MODEL), F32),
        scratch_shapes=[pltpu.VMEM((tm, D_MODEL), BF16)],
        compiler_params=_params(("parallel", "arbitrary")),
    )(x2, z, oa, norm_g, w_gates, w_gates, wba, wbs, wo)


def _mlp_kernel(x_ref, g_ref, wu_ref, wd_ref, fg_ref, y_ref, h_ref, *, final_norm):
    j = pl.program_id(1)

    @pl.when(j == 0)
    def _():
        x = x_ref[...]
        y_ref[...] = x
        h_ref[...] = _rms(x, g_ref[...]).astype(BF16)

    u = jnp.dot(h_ref[...], wu_ref[...], preferred_element_type=F32)
    u = jnp.square(jnp.maximum(u, 0.0)).astype(BF16)
    y_ref[...] += jnp.dot(u, wd_ref[...], preferred_element_type=F32)

    if final_norm:
        @pl.when(j == pl.num_programs(1) - 1)
        def _():
            y_ref[...] = _rms(y_ref[...], fg_ref[...])


def _mlp(x2, norm_g, wu, wd, final_g, final_norm, tm, tf):
    T = x2.shape[0]
    return pl.pallas_call(
        functools.partial(_mlp_kernel, final_norm=final_norm),
        grid=(T // tm, D_FF // tf),
        in_specs=[
            pl.BlockSpec((tm, D_MODEL), lambda i, j: (i, 0)),
            pl.BlockSpec((1, D_MODEL), lambda i, j: (0, 0)),
            pl.BlockSpec((D_MODEL, tf), lambda i, j: (0, j)),
            pl.BlockSpec((tf, D_MODEL), lambda i, j: (j, 0)),
            pl.BlockSpec((1, D_MODEL), lambda i, j: (0, 0)),
        ],
        out_specs=pl.BlockSpec((tm, D_MODEL), lambda i, j: (i, 0)),
        out_shape=jax.ShapeDtypeStruct((T, D_MODEL), F32),
        scratch_shapes=[pltpu.VMEM((tm, D_MODEL), BF16)],
        compiler_params=_params(("parallel", "arbitrary")),
    )(x2, norm_g, wu, wd, final_g)


def _rotary_lane_order(w):
    split = ROT_DIM + DH // 2 - ROT_HALF
    return jnp.concatenate([w[..., :ROT_HALF], w[..., ROT_DIM:split],
                            w[..., ROT_HALF:ROT_DIM], w[..., split:]], axis=-1)


def _rot_tables(seq_len):
    pos = jnp.arange(seq_len, dtype=F32)
    inv_freq = 1.0 / (jnp.float32(ROPE_THETA) ** (jnp.arange(0, ROT_DIM, 2, dtype=F32) / ROT_DIM))
    ang = pos[:, None] * inv_freq[None, :]
    cos, sin = jnp.cos(ang), jnp.sin(ang)
    pad = DH // 2 - ROT_HALF
    ones, zeros = jnp.ones((seq_len, pad), F32), jnp.zeros((seq_len, pad), F32)
    rc = jnp.concatenate([cos, ones, cos, ones], axis=-1)
    rs = jnp.concatenate([-sin, zeros, sin, zeros], axis=-1)
    return rc * EXP2_SCALE, rs * EXP2_SCALE, rc, rs


def _token_tile(seq_len):
    return min(512, seq_len // 2)


def _prepare(p):
    row = lambda a: a.reshape(1, -1).astype(F32)
    layers = []
    for l in range(p["w_in"].shape[0]):
        w_in = p["w_in"][l]
        w_qk = _rotary_lane_order(w_in[:, :2 * ATTN_W].reshape(D_MODEL, -1, DH))
        w_proj = jnp.concatenate([w_qk.reshape(D_MODEL, 2 * ATTN_W),
                                  w_in[:, 2 * ATTN_W:PROJ_COLS]], axis=1).astype(BF16)
        layers.append(dict(
            attn_norm_g=row(p["attn_norm_g"][l]), w_proj=w_proj,
            w_gates=w_in[:, PROJ_COLS:].astype(BF16),
            lq1=row(p["lambda_q1"][l]), lk1=row(p["lambda_k1"][l]),
            lq2=row(p["lambda_q2"][l]), lk2=row(p["lambda_k2"][l]),
            subln_g=row(p["subln_g"][l]), ln_g=row(p["sgu_ln_g"][l]), ln_b=row(p["sgu_ln_b"][l]),
            sgu_w=p["sgu_w"][l].astype(BF16), sgu_bt=p["sgu_b"][l].T.astype(F32),
            w_br_attn=p["w_br_attn"][l].astype(BF16), w_br_sgu=p["w_br_sgu"][l].astype(BF16),
            w_out=p["w_out"][l].astype(BF16), mlp_norm_g=row(p["mlp_norm_g"][l]),
            w_up=p["w_up"][l].astype(BF16), w_down=p["w_down"][l].astype(BF16)))
    return layers, row(p["final_norm_g"])


def _trunk(x, layers, final_g):
    B, S, _ = x.shape
    tm = _token_tile(S)
    rot = _rot_tables(S)
    x2 = x.reshape(B * S, D_MODEL)
    for l, w in enumerate(layers):
        qt4, vt4, z = _in_proj(x2, B, S, w["attn_norm_g"], w["w_proj"], rot, w["ln_g"], w["ln_b"],
                               w["sgu_w"], w["sgu_bt"], tm)
        oa = _attention(qt4, z.reshape(B, S, ZCOLS), vt4, w["lq1"], w["lk1"], w["lq2"], w["lk2"],
                        w["subln_g"], _lambda_init(l))
        x2 = _merge(x2, z, oa.reshape(B * S, ATTN_W), w["attn_norm_g"], w["w_gates"],
                    w["w_br_attn"], w["w_br_sgu"], w["w_out"], tm, MERGE_TN)
        x2 = _mlp(x2, w["mlp_norm_g"], w["w_up"], w["w_down"], final_g,
                  l == len(layers) - 1, tm, MLP_TF)
    return x2.reshape(B, S, D_MODEL)


def kernel(x_prompt, x_sample, attn_norm_g, w_in, lambda_q1, lambda_k1, lambda_q2, lambda_k2,
           subln_g, sgu_ln_g, sgu_ln_b, sgu_w, sgu_b, w_br_attn, w_br_sgu, w_out, mlp_norm_g,
           w_up, w_down, final_norm_g):
    p = dict(attn_norm_g=attn_norm_g, w_in=w_in, lambda_q1=lambda_q1, lambda_k1=lambda_k1,
             lambda_q2=lambda_q2, lambda_k2=lambda_k2, subln_g=subln_g, sgu_ln_g=sgu_ln_g,
             sgu_ln_b=sgu_ln_b, sgu_w=sgu_w, sgu_b=sgu_b, w_br_attn=w_br_attn, w_br_sgu=w_br_sgu,
             w_out=w_out, mlp_norm_g=mlp_norm_g, w_up=w_up, w_down=w_down,
             final_norm_g=final_norm_g)
    layers, final_g = _prepare(p)
    return _trunk(x_prompt, layers, final_g), _trunk(x_sample, layers, final_g)
```

```python
import functools
import math

import jax
import jax.numpy as jnp
from jax import lax
from jax.experimental import pallas as pl
from jax.experimental.pallas import tpu as pltpu

D_MODEL = 2048
N_HEADS = 4
DH = 128
HEAD_W = 2 * DH
ATTN_W = N_HEADS * HEAD_W
ONES_ROWS = 16
VT_ROWS = HEAD_W + ONES_ROWS
ROT_DIM = DH // 4
ROT_HALF = ROT_DIM // 2
ROPE_THETA = 500000.0
ATTN_SCALE = DH ** -0.5
CHUNK = 128
SGU_GROUPS = 8
SGU_W = SGU_GROUPS * CHUNK
D_FF = 4 * D_MODEL
PROJ_COLS = 3 * ATTN_W + 2 * SGU_W
ZCOLS = ATTN_W + SGU_W
RMS_EPS = 1e-6
LN_EPS = 1e-5
SQRT_HALF = math.sqrt(0.5)
EXP2_SCALE = ATTN_SCALE * math.log2(math.e)

VMEM_LIMIT_BYTES_V7X = 56 * 1024 * 1024

MERGE_TN = 512
MLP_TF = 1024

F32 = jnp.float32
BF16 = jnp.bfloat16


def _lambda_init(layer_idx):
    return 0.8 - 0.6 * math.exp(-0.3 * layer_idx)


def _params(semantics):
    return pltpu.CompilerParams(dimension_semantics=semantics,
                                vmem_limit_bytes=VMEM_LIMIT_BYTES_V7X)


def _rms(x, g):
    ms = jnp.mean(x * x, axis=-1, keepdims=True)
    return x * lax.rsqrt(ms + RMS_EPS) * g


def _gelu(x):
    return 0.5 * x * (1.0 + lax.erf(x * SQRT_HALF))


def _in_proj_kernel(x_ref, g_ref, w_ref, rc_ref, rs_ref, lng_ref, lnb_ref,
                    sw_ref, sbt_ref, qt_ref, vt_ref, z_ref, h_ref, su_ref, svn_ref):
    tm = x_ref.shape[0]
    h_ref[...] = _rms(x_ref[...], g_ref[...]).astype(BF16)

    def proj(c):
        return jnp.dot(h_ref[...], w_ref[:, c * ATTN_W:(c + 1) * ATTN_W],
                       preferred_element_type=F32)

    def rotary(acc, rc, rs, u):
        t = acc[:, u * DH:(u + 1) * DH]
        return t * rc + pltpu.roll(t, DH // 2, 1) * rs

    rc, rs = rc_ref[...], rs_ref[...]
    acc = proj(0)
    for u in range(ATTN_W // DH):
        qt_ref[u * DH:(u + 1) * DH, :] = (rotary(acc, rc, rs, u) * EXP2_SCALE).T.astype(BF16)

    acc = proj(1)
    for u in range(ATTN_W // DH):
        z_ref[:, u * DH:(u + 1) * DH] = rotary(acc, rc, rs, u).astype(BF16)

    vt = proj(2).T.astype(BF16)
    for hd in range(N_HEADS):
        vt_ref[hd * VT_ROWS:hd * VT_ROWS + HEAD_W, :] = vt[hd * HEAD_W:(hd + 1) * HEAD_W, :]
        vt_ref[hd * VT_ROWS + HEAD_W:(hd + 1) * VT_ROWS, :] = jnp.ones((ONES_ROWS, tm), BF16)

    su_ref[...] = _gelu(proj(3))
    y = _gelu(proj(4))
    mu = jnp.mean(y, axis=-1, keepdims=True)
    yc = y - mu
    var = jnp.mean(yc * yc, axis=-1, keepdims=True)
    svn_ref[...] = (yc * lax.rsqrt(var + LN_EPS) * lng_ref[...] + lnb_ref[...]).astype(BF16)

    for r in range(tm // CHUNK):
        rows = slice(r * CHUNK, (r + 1) * CHUNK)
        for g in range(SGU_GROUPS):
            cols = slice(g * CHUNK, (g + 1) * CHUNK)
            mixed = jnp.dot(sw_ref[g], svn_ref[rows, cols], preferred_element_type=F32)
            mixed = mixed + sbt_ref[:, g:g + 1]
            out_cols = slice(ATTN_W + g * CHUNK, ATTN_W + (g + 1) * CHUNK)
            z_ref[rows, out_cols] = (su_ref[rows, cols] * mixed).astype(BF16)


def _in_proj(x2, batch, seq_len, norm_g, w_proj, rot, ln_g, ln_b, sgu_w_bf, sgu_bt, tm):
    T = x2.shape[0]
    n_pos_blocks = seq_len // tm
    rot_spec = pl.BlockSpec((tm, DH), lambda i: (i % n_pos_blocks, 0))
    whole = lambda a: pl.BlockSpec(a.shape, lambda i: (0,) * a.ndim)
    t_spec = lambda rows: pl.BlockSpec((None, None, rows, tm),
                                       lambda i: (i // n_pos_blocks, i % n_pos_blocks, 0, 0))
    t_shape = lambda rows: jax.ShapeDtypeStruct((batch, n_pos_blocks, rows, tm), BF16)
    return pl.pallas_call(
        _in_proj_kernel,
        grid=(T // tm,),
        in_specs=[
            pl.BlockSpec((tm, D_MODEL), lambda i: (i, 0)),
            whole(norm_g), whole(w_proj),
            rot_spec, rot_spec,
            whole(ln_g), whole(ln_b), whole(sgu_w_bf), whole(sgu_bt),
        ],
        out_specs=[t_spec(ATTN_W), t_spec(N_HEADS * VT_ROWS),
                   pl.BlockSpec((tm, ZCOLS), lambda i: (i, 0)),
                   pl.BlockSpec((tm, D_MODEL), lambda i: (i, 0))],
        out_shape=[t_shape(ATTN_W), t_shape(N_HEADS * VT_ROWS),
                   jax.ShapeDtypeStruct((T, ZCOLS), BF16),
                   jax.ShapeDtypeStruct((T, D_MODEL), BF16)],
        scratch_shapes=[pltpu.VMEM((tm, SGU_W), F32),
                        pltpu.VMEM((tm, SGU_W), BF16)],
        compiler_params=_params(("parallel",)),
    )(x2, norm_g, w_proj, rot[0], rot[1], ln_g, ln_b, sgu_w_bf, sgu_bt)


def _attn_kernel(qt_ref, k_ref, vt_ref, lq1_ref, lk1_ref, lq2_ref, lk2_ref, g_ref, o_ref,
                 s0_ref, s1_ref, p0_ref, p1_ref, acc_ref, *, tk, n_kv, lambda_init):
    s_refs, p_refs = (s0_ref, s1_ref), (p0_ref, p1_ref)

    def scores(j, slot):
        rows = pl.ds(pl.multiple_of(j * tk, tk), tk)
        col_max = []
        for c in range(2):
            s = jnp.dot(k_ref[rows, c * DH:(c + 1) * DH], qt_ref[c * DH:(c + 1) * DH, :],
                        preferred_element_type=F32)
            s_refs[slot][c] = s
            col_max.append(jnp.max(s, axis=0, keepdims=True))
        return tuple(col_max)

    def softmax(slot, col_max, run_max):
        new_max, alphas = [], []
        for c in range(2):
            m_new = jnp.maximum(run_max[c], col_max[c])
            alphas.append(jnp.exp2(run_max[c] - m_new))
            p_refs[slot][c] = jnp.exp2(s_refs[slot][c] - m_new).astype(BF16)
            new_max.append(m_new)
        return tuple(new_max), tuple(alphas)

    def values(j, slot, alphas):
        vt = vt_ref[j]
        for c in range(2):
            pv = jnp.dot(vt, p_refs[slot][c], preferred_element_type=F32)
            acc_ref[c] = alphas[c] * acc_ref[c] + pv

    tq = qt_ref.shape[1]
    acc_ref[...] = jnp.zeros(acc_ref.shape, F32)
    run_max = tuple(jnp.full((1, tq), -jnp.inf, F32) for _ in range(2))

    col_max = scores(0, 0)
    col_max_next = scores(1, 1)
    run_max, alphas = softmax(0, col_max, run_max)

    def two_chunks(u, carry):
        col_max, run_max, alphas = carry
        col_max_a = scores(2 * u + 2, 0)
        run_max, alphas_a = softmax(1, col_max, run_max)
        values(2 * u, 0, alphas)
        col_max_b = scores(2 * u + 3, 1)
        run_max, alphas_b = softmax(0, col_max_a, run_max)
        values(2 * u + 1, 1, alphas_a)
        return col_max_b, run_max, alphas_b

    col_max, run_max, alphas = lax.fori_loop(0, n_kv // 2 - 1, two_chunks,
                                             (col_max_next, run_max, alphas))
    _, alphas_last = softmax(1, col_max, run_max)
    values(n_kv - 2, 0, alphas)
    values(n_kv - 1, 1, alphas_last)

    l1, l2 = acc_ref[0, HEAD_W:HEAD_W + 1, :], acc_ref[1, HEAD_W:HEAD_W + 1, :]
    lam = (jnp.exp(jnp.sum(lq1_ref[...] * lk1_ref[...], axis=-1, keepdims=True))
           - jnp.exp(jnp.sum(lq2_ref[...] * lk2_ref[...], axis=-1, keepdims=True))
           + lambda_init)
    o_t = (acc_ref[0, :HEAD_W, :] * (1.0 / l1)
           - lam * (acc_ref[1, :HEAD_W, :] * (1.0 / l2)))
    ms = jnp.mean(o_t * o_t, axis=0, keepdims=True)
    y = (o_t * lax.rsqrt(ms + RMS_EPS)).T
    o_ref[...] = (y * g_ref[...] * (1.0 - lambda_init)).astype(BF16)


def _attention(qt4, z3, vt4, lq1, lk1, lq2, lk2, subln_g, lambda_init):
    B, n_kv, _, tk = vt4.shape
    n_q, tq = qt4.shape[1], qt4.shape[3]
    S = z3.shape[1]
    assert n_kv >= 2 and n_kv % 2 == 0, "the chunk pipeline takes an even number of key chunks"
    vec = lambda n: pl.BlockSpec((1, n), lambda b, h, i: (0, 0))
    kern = functools.partial(_attn_kernel, tk=tk, n_kv=n_kv, lambda_init=lambda_init)
    slot = lambda dt: pltpu.VMEM((2, tk, tq), dt)
    return pl.pallas_call(
        kern,
        grid=(B, N_HEADS, n_q),
        in_specs=[
            pl.BlockSpec((None, None, HEAD_W, tq), lambda b, h, i: (b, i, h, 0)),
            pl.BlockSpec((None, S, HEAD_W), lambda b, h, i: (b, 0, h)),
            pl.BlockSpec((None, n_kv, VT_ROWS, tk), lambda b, h, i: (b, 0, h, 0)),
            vec(DH), vec(DH), vec(DH), vec(DH),
            vec(HEAD_W),
        ],
        out_specs=pl.BlockSpec((None, tq, HEAD_W), lambda b, h, i: (b, i, h)),
        out_shape=jax.ShapeDtypeStruct((B, S, ATTN_W), BF16),
        scratch_shapes=[slot(F32), slot(F32), slot(BF16), slot(BF16),
                        pltpu.VMEM((2, VT_ROWS, tq), F32)],
        compiler_params=_params(("parallel", "parallel", "arbitrary")),
    )(qt4, z3, vt4, lq1, lk1, lq2, lk2, subln_g)


def _merge_kernel(x_ref, h_ref, osgu_ref, oa_ref, wga_ref, wgs_ref, wba_ref, wbs_ref, wo_ref,
                  y_ref):
    @pl.when(pl.program_id(1) == 0)
    def _():
        y_ref[...] = x_ref[...]

    h = h_ref[...]
    ga = jax.nn.sigmoid(jnp.dot(h, wga_ref[...], preferred_element_type=F32))
    gs = jax.nn.sigmoid(jnp.dot(h, wgs_ref[...], preferred_element_type=F32))
    ba = jnp.dot(oa_ref[...], wba_ref[...], preferred_element_type=F32)
    bs = jnp.dot(osgu_ref[...], wbs_ref[...], preferred_element_type=F32)
    merged = (ga * ba + gs * bs).astype(BF16)
    y_ref[...] += jnp.dot(merged, wo_ref[...], preferred_element_type=F32)


def _merge(x2, h, z, oa, w_gates, wba, wbs, wo, tm, tn):
    T = x2.shape[0]
    return pl.pallas_call(
        _merge_kernel,
        grid=(T // tm, D_MODEL // tn),
        in_specs=[
            pl.BlockSpec((tm, D_MODEL), lambda i, j: (i, 0)),
            pl.BlockSpec((tm, D_MODEL), lambda i, j: (i, 0)),
            pl.BlockSpec((tm, SGU_W), lambda i, j: (i, ATTN_W // SGU_W)),
            pl.BlockSpec((tm, ATTN_W), lambda i, j: (i, 0)),
            pl.BlockSpec((D_MODEL, tn), lambda i, j: (0, j)),
            pl.BlockSpec((D_MODEL, tn), lambda i, j: (0, D_MODEL // tn + j)),
            pl.BlockSpec((ATTN_W, tn), lambda i, j: (0, j)),
            pl.BlockSpec((SGU_W, tn), lambda i, j: (0, j)),
            pl.BlockSpec((tn, D_MODEL), lambda i, j: (j, 0)),
        ],
        out_specs=pl.BlockSpec((tm, D_MODEL), lambda i, j: (i, 0)),
        out_shape=jax.ShapeDtypeStruct((T, D_MODEL), F32),
        compiler_params=_params(("parallel", "arbitrary")),
    )(x2, h, z, oa, w_gates, w_gates, wba, wbs, wo)


def _mlp_kernel(x_ref, g_ref, wu_ref, wd_ref, fg_ref, y_ref, h_ref, *, final_norm):
    j = pl.program_id(1)

    @pl.when(j == 0)
    def _():
        x = x_ref[...]
        y_ref[...] = x
        h_ref[...] = _rms(x, g_ref[...]).astype(BF16)

    u = jnp.dot(h_ref[...], wu_ref[...], preferred_element_type=F32)
    u = jnp.square(jnp.maximum(u, 0.0)).astype(BF16)
    y_ref[...] += jnp.dot(u, wd_ref[...], preferred_element_type=F32)

    if final_norm:
        @pl.when(j == pl.num_programs(1) - 1)
        def _():
            y_ref[...] = _rms(y_ref[...], fg_ref[...])


def _mlp(x2, norm_g, wu, wd, final_g, final_norm, tm, tf):
    T = x2.shape[0]
    return pl.pallas_call(
        functools.partial(_mlp_kernel, final_norm=final_norm),
        grid=(T // tm, D_FF // tf),
        in_specs=[
            pl.BlockSpec((tm, D_MODEL), lambda i, j: (i, 0)),
            pl.BlockSpec((1, D_MODEL), lambda i, j: (0, 0)),
            pl.BlockSpec((D_MODEL, tf), lambda i, j: (0, j)),
            pl.BlockSpec((tf, D_MODEL), lambda i, j: (j, 0)),
            pl.BlockSpec((1, D_MODEL), lambda i, j: (0, 0)),
        ],
        out_specs=pl.BlockSpec((tm, D_MODEL), lambda i, j: (i, 0)),
        out_shape=jax.ShapeDtypeStruct((T, D_MODEL), F32),
        scratch_shapes=[pltpu.VMEM((tm, D_MODEL), BF16)],
        compiler_params=_params(("parallel", "arbitrary")),
    )(x2, norm_g, wu, wd, final_g)


def _rotary_lane_order(w):
    split = ROT_DIM + DH // 2 - ROT_HALF
    return jnp.concatenate([w[..., :ROT_HALF], w[..., ROT_DIM:split],
                            w[..., ROT_HALF:ROT_DIM], w[..., split:]], axis=-1)


def _rot_tables(seq_len):
    pos = jnp.arange(seq_len, dtype=F32)
    inv_freq = 1.0 / (jnp.float32(ROPE_THETA) ** (jnp.arange(0, ROT_DIM, 2, dtype=F32) / ROT_DIM))
    ang = pos[:, None] * inv_freq[None, :]
    cos, sin = jnp.cos(ang), jnp.sin(ang)
    pad = DH // 2 - ROT_HALF
    ones, zeros = jnp.ones((seq_len, pad), F32), jnp.zeros((seq_len, pad), F32)
    rc = jnp.concatenate([cos, ones, cos, ones], axis=-1)
    rs = jnp.concatenate([-sin, zeros, sin, zeros], axis=-1)
    return rc, rs


def _token_tile(seq_len):
    return min(512, seq_len // 2)


def _prepare(p):
    row = lambda a: a.reshape(1, -1).astype(F32)
    layers = []
    for l in range(p["w_in"].shape[0]):
        w_in = p["w_in"][l]
        w_qk = _rotary_lane_order(w_in[:, :2 * ATTN_W].reshape(D_MODEL, -1, DH))
        w_proj = jnp.concatenate([w_qk.reshape(D_MODEL, 2 * ATTN_W),
                                  w_in[:, 2 * ATTN_W:PROJ_COLS]], axis=1).astype(BF16)
        layers.append(dict(
            attn_norm_g=row(p["attn_norm_g"][l]), w_proj=w_proj,
            w_gates=w_in[:, PROJ_COLS:].astype(BF16),
            lq1=row(p["lambda_q1"][l]), lk1=row(p["lambda_k1"][l]),
            lq2=row(p["lambda_q2"][l]), lk2=row(p["lambda_k2"][l]),
            subln_g=row(p["subln_g"][l]), ln_g=row(p["sgu_ln_g"][l]), ln_b=row(p["sgu_ln_b"][l]),
            sgu_w=p["sgu_w"][l].astype(BF16), sgu_bt=p["sgu_b"][l].T.astype(F32),
            w_br_attn=p["w_br_attn"][l].astype(BF16), w_br_sgu=p["w_br_sgu"][l].astype(BF16),
            w_out=p["w_out"][l].astype(BF16), mlp_norm_g=row(p["mlp_norm_g"][l]),
            w_up=p["w_up"][l].astype(BF16), w_down=p["w_down"][l].astype(BF16)))
    return layers, row(p["final_norm_g"])


def _trunk(x, layers, final_g, rot):
    B, S, _ = x.shape
    tm = _token_tile(S)
    x2 = x.reshape(B * S, D_MODEL)
    for l, w in enumerate(layers):
        qt4, vt4, z, h = _in_proj(x2, B, S, w["attn_norm_g"], w["w_proj"], rot, w["ln_g"],
                                  w["ln_b"], w["sgu_w"], w["sgu_bt"], tm)
        oa = _attention(qt4, z.reshape(B, S, ZCOLS), vt4, w["lq1"], w["lk1"], w["lq2"], w["lk2"],
                        w["subln_g"], _lambda_init(l))
        x2 = _merge(x2, h, z, oa.reshape(B * S, ATTN_W), w["w_gates"],
                    w["w_br_attn"], w["w_br_sgu"], w["w_out"], tm, MERGE_TN)
        x2 = _mlp(x2, w["mlp_norm_g"], w["w_up"], w["w_down"], final_g,
                  l == len(layers) - 1, tm, MLP_TF)
    return x2.reshape(B, S, D_MODEL)


def kernel(x_prompt, x_sample, attn_norm_g, w_in, lambda_q1, lambda_k1, lambda_q2, lambda_k2,
           subln_g, sgu_ln_g, sgu_ln_b, sgu_w, sgu_b, w_br_attn, w_br_sgu, w_out, mlp_norm_g,
           w_up, w_down, final_norm_g):
    p = dict(attn_norm_g=attn_norm_g, w_in=w_in, lambda_q1=lambda_q1, lambda_k1=lambda_k1,
             lambda_q2=lambda_q2, lambda_k2=lambda_k2, subln_g=subln_g, sgu_ln_g=sgu_ln_g,
             sgu_ln_b=sgu_ln_b, sgu_w=sgu_w, sgu_b=sgu_b, w_br_attn=w_br_attn, w_br_sgu=w_br_sgu,
             w_out=w_out, mlp_norm_g=mlp_norm_g, w_up=w_up, w_down=w_down,
             final_norm_g=final_norm_g)
    layers, final_g = _prepare(p)
    rot = _rot_tables(max(x_prompt.shape[1], x_sample.shape[1]))
    return _trunk(x_prompt, layers, final_g, rot), _trunk(x_sample, layers, final_g, rot)
```

```python
import functools
import math

import jax
import jax.numpy as jnp
from jax import lax
from jax.experimental import pallas as pl
from jax.experimental.pallas import tpu as pltpu

D_MODEL = 2048
N_HEADS = 4
DH = 128
HEAD_W = 2 * DH
ATTN_W = N_HEADS * HEAD_W
ONES_ROWS = 16
VT_ROWS = HEAD_W + ONES_ROWS
ROT_DIM = DH // 4
ROT_HALF = ROT_DIM // 2
ROPE_THETA = 500000.0
ATTN_SCALE = DH ** -0.5
CHUNK = 128
SGU_GROUPS = 8
SGU_W = SGU_GROUPS * CHUNK
D_FF = 4 * D_MODEL
PROJ_COLS = 3 * ATTN_W + 2 * SGU_W
ZCOLS = ATTN_W + SGU_W
RMS_EPS = 1e-6
LN_EPS = 1e-5
SQRT_HALF = math.sqrt(0.5)
EXP2_SCALE = ATTN_SCALE * math.log2(math.e)

VMEM_LIMIT_BYTES_V7X = 56 * 1024 * 1024

ATTN_MIN_PAIRS = 8
MERGE_TN = 512
MLP_TF = 1024

F32 = jnp.float32
BF16 = jnp.bfloat16


def _lambda_init(layer_idx):
    return 0.8 - 0.6 * math.exp(-0.3 * layer_idx)


def _params(semantics):
    return pltpu.CompilerParams(dimension_semantics=semantics,
                                vmem_limit_bytes=VMEM_LIMIT_BYTES_V7X)


def _rms(x, g):
    ms = jnp.mean(x * x, axis=-1, keepdims=True)
    return x * lax.rsqrt(ms + RMS_EPS) * g


def _gelu(x):
    return 0.5 * x * (1.0 + lax.erf(x * SQRT_HALF))


def _in_proj_kernel(x_ref, g_ref, w_ref, rc_ref, rs_ref, lng_ref, lnb_ref,
                    sw_ref, sbt_ref, qt_ref, vt_ref, z_ref, h_ref, su_ref, svn_ref):
    tm = x_ref.shape[0]
    h_ref[...] = _rms(x_ref[...], g_ref[...]).astype(BF16)

    def proj(c):
        return jnp.dot(h_ref[...], w_ref[:, c * ATTN_W:(c + 1) * ATTN_W],
                       preferred_element_type=F32)

    def rotary(acc, rc, rs, u):
        t = acc[:, u * DH:(u + 1) * DH]
        return t * rc + pltpu.roll(t, DH // 2, 1) * rs

    rc, rs = rc_ref[...], rs_ref[...]
    acc = proj(0)
    for u in range(ATTN_W // DH):
        qt_ref[u * DH:(u + 1) * DH, :] = (rotary(acc, rc, rs, u) * EXP2_SCALE).T.astype(BF16)

    acc = proj(1)
    for u in range(ATTN_W // DH):
        z_ref[:, u * DH:(u + 1) * DH] = rotary(acc, rc, rs, u).astype(BF16)

    vt = proj(2).T.astype(BF16)
    for hd in range(N_HEADS):
        vt_ref[hd * VT_ROWS:hd * VT_ROWS + HEAD_W, :] = vt[hd * HEAD_W:(hd + 1) * HEAD_W, :]
        vt_ref[hd * VT_ROWS + HEAD_W:(hd + 1) * VT_ROWS, :] = jnp.ones((ONES_ROWS, tm), BF16)

    su_ref[...] = _gelu(proj(3))
    y = _gelu(proj(4))
    mu = jnp.mean(y, axis=-1, keepdims=True)
    yc = y - mu
    var = jnp.mean(yc * yc, axis=-1, keepdims=True)
    svn_ref[...] = (yc * lax.rsqrt(var + LN_EPS) * lng_ref[...] + lnb_ref[...]).astype(BF16)

    for r in range(tm // CHUNK):
        rows = slice(r * CHUNK, (r + 1) * CHUNK)
        for g in range(SGU_GROUPS):
            cols = slice(g * CHUNK, (g + 1) * CHUNK)
            mixed = jnp.dot(sw_ref[g], svn_ref[rows, cols], preferred_element_type=F32)
            mixed = mixed + sbt_ref[:, g:g + 1]
            out_cols = slice(ATTN_W + g * CHUNK, ATTN_W + (g + 1) * CHUNK)
            z_ref[rows, out_cols] = (su_ref[rows, cols] * mixed).astype(BF16)


def _in_proj(x2, batch, seq_len, norm_g, w_proj, rot, ln_g, ln_b, sgu_w_bf, sgu_bt, tm):
    T = x2.shape[0]
    n_pos_blocks = seq_len // tm
    rot_spec = pl.BlockSpec((tm, DH), lambda i: (i % n_pos_blocks, 0))
    whole = lambda a: pl.BlockSpec(a.shape, lambda i: (0,) * a.ndim)
    t_spec = lambda rows: pl.BlockSpec((None, None, rows, tm),
                                       lambda i: (i // n_pos_blocks, i % n_pos_blocks, 0, 0))
    t_shape = lambda rows: jax.ShapeDtypeStruct((batch, n_pos_blocks, rows, tm), BF16)
    return pl.pallas_call(
        _in_proj_kernel,
        grid=(T // tm,),
        in_specs=[
            pl.BlockSpec((tm, D_MODEL), lambda i: (i, 0)),
            whole(norm_g), whole(w_proj),
            rot_spec, rot_spec,
            whole(ln_g), whole(ln_b), whole(sgu_w_bf), whole(sgu_bt),
        ],
        out_specs=[t_spec(ATTN_W), t_spec(N_HEADS * VT_ROWS),
                   pl.BlockSpec((tm, ZCOLS), lambda i: (i, 0))],
        out_shape=[t_shape(ATTN_W), t_shape(N_HEADS * VT_ROWS),
                   jax.ShapeDtypeStruct((T, ZCOLS), BF16)],
        scratch_shapes=[pltpu.VMEM((tm, D_MODEL), BF16),
                        pltpu.VMEM((tm, SGU_W), F32),
                        pltpu.VMEM((tm, SGU_W), BF16)],
        compiler_params=_params(("parallel",)),
    )(x2, norm_g, w_proj, rot[0], rot[1], ln_g, ln_b, sgu_w_bf, sgu_bt)


def _attn_kernel(qt_ref, k_ref, vt_ref, lq1_ref, lk1_ref, lq2_ref, lk2_ref, g_ref, o_ref,
                 s0_ref, s1_ref, p0_ref, p1_ref, acc_ref, *, tk, n_kv, lambda_init):
    s_refs, p_refs = (s0_ref, s1_ref), (p0_ref, p1_ref)
    q_tiles, _, tq = qt_ref.shape
    streams = [(c, t) for c in range(2) for t in range(q_tiles)]

    def scores(j, slot):
        rows = pl.ds(pl.multiple_of(j * tk, tk), tk)
        col_max = []
        for i, (c, t) in enumerate(streams):
            s = jnp.dot(k_ref[rows, c * DH:(c + 1) * DH], qt_ref[t, c * DH:(c + 1) * DH, :],
                        preferred_element_type=F32)
            s_refs[slot][i] = s
            col_max.append(jnp.max(s, axis=0, keepdims=True))
        return tuple(col_max)

    def softmax(slot, col_max, run_max):
        new_max, alphas = [], []
        for i in range(len(streams)):
            m_new = jnp.maximum(run_max[i], col_max[i])
            alphas.append(jnp.exp2(run_max[i] - m_new))
            p_refs[slot][i] = jnp.exp2(s_refs[slot][i] - m_new).astype(BF16)
            new_max.append(m_new)
        return tuple(new_max), tuple(alphas)

    def values(j, slot, alphas):
        vt = vt_ref[j]
        for i in range(len(streams)):
            pv = jnp.dot(vt, p_refs[slot][i], preferred_element_type=F32)
            acc_ref[i] = alphas[i] * acc_ref[i] + pv

    acc_ref[...] = jnp.zeros(acc_ref.shape, F32)
    run_max = tuple(jnp.full((1, tq), -jnp.inf, F32) for _ in streams)

    col_max = scores(0, 0)
    col_max_next = scores(1, 1)
    run_max, alphas = softmax(0, col_max, run_max)

    def two_chunks(u, carry):
        col_max, run_max, alphas = carry
        col_max_a = scores(2 * u + 2, 0)
        run_max, alphas_a = softmax(1, col_max, run_max)
        values(2 * u, 0, alphas)
        col_max_b = scores(2 * u + 3, 1)
        run_max, alphas_b = softmax(0, col_max_a, run_max)
        values(2 * u + 1, 1, alphas_a)
        return col_max_b, run_max, alphas_b

    col_max, run_max, alphas = lax.fori_loop(0, n_kv // 2 - 1, two_chunks,
                                             (col_max_next, run_max, alphas))
    _, alphas_last = softmax(1, col_max, run_max)
    values(n_kv - 2, 0, alphas)
    values(n_kv - 1, 1, alphas_last)

    lam = (jnp.exp(jnp.sum(lq1_ref[...] * lk1_ref[...], axis=-1, keepdims=True))
           - jnp.exp(jnp.sum(lq2_ref[...] * lk2_ref[...], axis=-1, keepdims=True))
           + lambda_init)
    for t in range(q_tiles):
        l1 = acc_ref[t, HEAD_W:HEAD_W + 1, :]
        l2 = acc_ref[q_tiles + t, HEAD_W:HEAD_W + 1, :]
        o_t = (acc_ref[t, :HEAD_W, :] * (1.0 / l1)
               - lam * (acc_ref[q_tiles + t, :HEAD_W, :] * (1.0 / l2)))
        ms = jnp.mean(o_t * o_t, axis=0, keepdims=True)
        y = (o_t * lax.rsqrt(ms + RMS_EPS)).T
        o_ref[t * tq:(t + 1) * tq, :] = (y * g_ref[...] * (1.0 - lambda_init)).astype(BF16)


def _attention(qt4, z3, vt4, lq1, lk1, lq2, lk2, subln_g, lambda_init):
    B, n_kv, _, tk = vt4.shape
    n_q, tq = qt4.shape[1], qt4.shape[3]
    S = z3.shape[1]
    assert n_kv >= 2 and n_kv % 2 == 0, "the chunk pipeline takes an even number of key chunks"
    q_tiles = min(n_q, max(1, ATTN_MIN_PAIRS // n_kv))
    assert n_q % q_tiles == 0
    vec = lambda n: pl.BlockSpec((1, n), lambda b, h, i: (0, 0))
    kern = functools.partial(_attn_kernel, tk=tk, n_kv=n_kv, lambda_init=lambda_init)
    slot = lambda dt: pltpu.VMEM((2 * q_tiles, tk, tq), dt)
    return pl.pallas_call(
        kern,
        grid=(B, N_HEADS, n_q // q_tiles),
        in_specs=[
            pl.BlockSpec((None, q_tiles, HEAD_W, tq), lambda b, h, i: (b, i, h, 0)),
            pl.BlockSpec((None, S, HEAD_W), lambda b, h, i: (b, 0, h)),
            pl.BlockSpec((None, n_kv, VT_ROWS, tk), lambda b, h, i: (b, 0, h, 0)),
            vec(DH), vec(DH), vec(DH), vec(DH),
            vec(HEAD_W),
        ],
        out_specs=pl.BlockSpec((None, q_tiles * tq, HEAD_W), lambda b, h, i: (b, i, h)),
        out_shape=jax.ShapeDtypeStruct((B, S, ATTN_W), BF16),
        scratch_shapes=[slot(F32), slot(F32), slot(BF16), slot(BF16),
                        pltpu.VMEM((2 * q_tiles, VT_ROWS, tq), F32)],
        compiler_params=_params(("parallel", "parallel", "arbitrary")),
    )(qt4, z3, vt4, lq1, lk1, lq2, lk2, subln_g)


def _merge_kernel(x_ref, osgu_ref, oa_ref, g_ref, wga_ref, wgs_ref, wba_ref, wbs_ref, wo_ref,
                  y_ref, h_ref):
    @pl.when(pl.program_id(1) == 0)
    def _():
        x = x_ref[...]
        y_ref[...] = x
        h_ref[...] = _rms(x, g_ref[...]).astype(BF16)

    h = h_ref[...]
    ga = jax.nn.sigmoid(jnp.dot(h, wga_ref[...], preferred_element_type=F32))
    gs = jax.nn.sigmoid(jnp.dot(h, wgs_ref[...], preferred_element_type=F32))
    ba = jnp.dot(oa_ref[...], wba_ref[...], preferred_element_type=F32)
    bs = jnp.dot(osgu_ref[...], wbs_ref[...], preferred_element_type=F32)
    merged = (ga * ba + gs * bs).astype(BF16)
    y_ref[...] += jnp.dot(merged, wo_ref[...], preferred_element_type=F32)


def _merge(x2, z, oa, norm_g, w_gates, wba, wbs, wo, tm, tn):
    T = x2.shape[0]
    return pl.pallas_call(
        _merge_kernel,
        grid=(T // tm, D_MODEL // tn),
        in_specs=[
            pl.BlockSpec((tm, D_MODEL), lambda i, j: (i, 0)),
            pl.BlockSpec((tm, SGU_W), lambda i, j: (i, ATTN_W // SGU_W)),
            pl.BlockSpec((tm, ATTN_W), lambda i, j: (i, 0)),
            pl.BlockSpec((1, D_MODEL), lambda i, j: (0, 0)),
            pl.BlockSpec((D_MODEL, tn), lambda i, j: (0, j)),
            pl.BlockSpec((D_MODEL, tn), lambda i, j: (0, D_MODEL // tn + j)),
            pl.BlockSpec((ATTN_W, tn), lambda i, j: (0, j)),
            pl.BlockSpec((SGU_W, tn), lambda i, j: (0, j)),
            pl.BlockSpec((tn, D_MODEL), lambda i, j: (j, 0)),
        ],
        out_specs=pl.BlockSpec((tm, D_MODEL), lambda i, j: (i, 0)),
        out_shape=jax.ShapeDtypeStruct((T, D_MODEL), F32),
        scratch_shapes=[pltpu.VMEM((tm, D_MODEL), BF16)],
        compiler_params=_params(("parallel", "arbitrary")),
    )(x2, z, oa, norm_g, w_gates, w_gates, wba, wbs, wo)


def _mlp_kernel(x_ref, g_ref, wu_ref, wd_ref, fg_ref, y_ref, h_ref, *, final_norm):
    j = pl.program_id(1)

    @pl.when(j == 0)
    def _():
        x = x_ref[...]
        y_ref[...] = x
        h_ref[...] = _rms(x, g_ref[...]).astype(BF16)

    u = jnp.dot(h_ref[...], wu_ref[...], preferred_element_type=F32)
    u = jnp.square(jnp.maximum(u, 0.0)).astype(BF16)
    y_ref[...] += jnp.dot(u, wd_ref[...], preferred_element_type=F32)

    if final_norm:
        @pl.when(j == pl.num_programs(1) - 1)
        def _():
            y_ref[...] = _rms(y_ref[...], fg_ref[...])


def _mlp(x2, norm_g, wu, wd, final_g, final_norm, tm, tf):
    T = x2.shape[0]
    return pl.pallas_call(
        functools.partial(_mlp_kernel, final_norm=final_norm),
        grid=(T // tm, D_FF // tf),
        in_specs=[
            pl.BlockSpec((tm, D_MODEL), lambda i, j: (i, 0)),
            pl.BlockSpec((1, D_MODEL), lambda i, j: (0, 0)),
            pl.BlockSpec((D_MODEL, tf), lambda i, j: (0, j)),
            pl.BlockSpec((tf, D_MODEL), lambda i, j: (j, 0)),
            pl.BlockSpec((1, D_MODEL), lambda i, j: (0, 0)),
        ],
        out_specs=pl.BlockSpec((tm, D_MODEL), lambda i, j: (i, 0)),
        out_shape=jax.ShapeDtypeStruct((T, D_MODEL), F32),
        scratch_shapes=[pltpu.VMEM((tm, D_MODEL), BF16)],
        compiler_params=_params(("parallel", "arbitrary")),
    )(x2, norm_g, wu, wd, final_g)


def _rotary_lane_order(w):
    split = ROT_DIM + DH // 2 - ROT_HALF
    return jnp.concatenate([w[..., :ROT_HALF], w[..., ROT_DIM:split],
                            w[..., ROT_HALF:ROT_DIM], w[..., split:]], axis=-1)


def _rot_tables(seq_len):
    pos = jnp.arange(seq_len, dtype=F32)
    inv_freq = 1.0 / (jnp.float32(ROPE_THETA) ** (jnp.arange(0, ROT_DIM, 2, dtype=F32) / ROT_DIM))
    ang = pos[:, None] * inv_freq[None, :]
    cos, sin = jnp.cos(ang), jnp.sin(ang)
    pad = DH // 2 - ROT_HALF
    ones, zeros = jnp.ones((seq_len, pad), F32), jnp.zeros((seq_len, pad), F32)
    rc = jnp.concatenate([cos, ones, cos, ones], axis=-1)
    rs = jnp.concatenate([-sin, zeros, sin, zeros], axis=-1)
    return rc, rs


def _token_tile(seq_len):
    return min(512, seq_len // 2)


def _prepare(p):
    row = lambda a: a.reshape(1, -1).astype(F32)
    layers = []
    for l in range(p["w_in"].shape[0]):
        w_in = p["w_in"][l]
        w_qk = _rotary_lane_order(w_in[:, :2 * ATTN_W].reshape(D_MODEL, -1, DH))
        w_proj = jnp.concatenate([w_qk.reshape(D_MODEL, 2 * ATTN_W),
                                  w_in[:, 2 * ATTN_W:PROJ_COLS]], axis=1).astype(BF16)
        layers.append(dict(
            attn_norm_g=row(p["attn_norm_g"][l]), w_proj=w_proj,
            w_gates=w_in[:, PROJ_COLS:].astype(BF16),
            lq1=row(p["lambda_q1"][l]), lk1=row(p["lambda_k1"][l]),
            lq2=row(p["lambda_q2"][l]), lk2=row(p["lambda_k2"][l]),
            subln_g=row(p["subln_g"][l]), ln_g=row(p["sgu_ln_g"][l]), ln_b=row(p["sgu_ln_b"][l]),
            sgu_w=p["sgu_w"][l].astype(BF16), sgu_bt=p["sgu_b"][l].T.astype(F32),
            w_br_attn=p["w_br_attn"][l].astype(BF16), w_br_sgu=p["w_br_sgu"][l].astype(BF16),
            w_out=p["w_out"][l].astype(BF16), mlp_norm_g=row(p["mlp_norm_g"][l]),
            w_up=p["w_up"][l].astype(BF16), w_down=p["w_down"][l].astype(BF16)))
    return layers, row(p["final_norm_g"])


def _trunk(x, layers, final_g, rot):
    B, S, _ = x.shape
    tm = _token_tile(S)
    x2 = x.reshape(B * S, D_MODEL)
    for l, w in enumerate(layers):
        qt4, vt4, z = _in_proj(x2, B, S, w["attn_norm_g"], w["w_proj"], rot, w["ln_g"], w["ln_b"],
                               w["sgu_w"], w["sgu_bt"], tm)
        oa = _attention(qt4, z.reshape(B, S, ZCOLS), vt4, w["lq1"], w["lk1"], w["lq2"], w["lk2"],
                        w["subln_g"], _lambda_init(l))
        x2 = _merge(x2, z, oa.reshape(B * S, ATTN_W), w["attn_norm_g"], w["w_gates"],
                    w["w_br_attn"], w["w_br_sgu"], w["w_out"], tm, MERGE_TN)
        x2 = _mlp(x2, w["mlp_norm_g"], w["w_up"], w["w_down"], final_g,
                  l == len(layers) - 1, tm, MLP_TF)
    return x2.reshape(B, S, D_MODEL)


def kernel(x_prompt, x_sample, attn_norm_g, w_in, lambda_q1, lambda_k1, lambda_q2, lambda_k2,
           subln_g, sgu_ln_g, sgu_ln_b, sgu_w, sgu_b, w_br_attn, w_br_sgu, w_out, mlp_norm_g,
           w_up, w_down, final_norm_g):
    p = dict(attn_norm_g=attn_norm_g, w_in=w_in, lambda_q1=lambda_q1, lambda_k1=lambda_k1,
             lambda_q2=lambda_q2, lambda_k2=lambda_k2, subln_g=subln_g, sgu_ln_g=sgu_ln_g,
             sgu_ln_b=sgu_ln_b, sgu_w=sgu_w, sgu_b=sgu_b, w_br_attn=w_br_attn, w_br_sgu=w_br_sgu,
             w_out=w_out, mlp_norm_g=mlp_norm_g, w_up=w_up, w_down=w_down,
             final_norm_g=final_norm_g)
    layers, final_g = _prepare(p)
    rot = _rot_tables(max(x_prompt.shape[1], x_sample.shape[1]))
    return _trunk(x_prompt, layers, final_g, rot), _trunk(x_sample, layers, final_g, rot)
```

```python
import functools
import math

import jax
import jax.numpy as jnp
from jax import lax
from jax.experimental import pallas as pl
from jax.experimental.pallas import tpu as pltpu

D_MODEL = 2048
N_HEADS = 4
DH = 128
HEAD_W = 2 * DH
ATTN_W = N_HEADS * HEAD_W
ONES_ROWS = 16
VT_ROWS = HEAD_W + ONES_ROWS
ROT_DIM = DH // 4
ROT_HALF = ROT_DIM // 2
ROPE_THETA = 500000.0
ATTN_SCALE = DH ** -0.5
CHUNK = 128
SGU_GROUPS = 8
SGU_W = SGU_GROUPS * CHUNK
D_FF = 4 * D_MODEL
PROJ_COLS = 3 * ATTN_W + 2 * SGU_W
ZCOLS = ATTN_W + SGU_W
RMS_EPS = 1e-6
LN_EPS = 1e-5
SQRT_HALF = math.sqrt(0.5)
EXP2_SCALE = ATTN_SCALE * math.log2(math.e)

VMEM_LIMIT_BYTES_V7X = 60 * 1024 * 1024

WIDE_TM = 1024
MERGE_TN = 256
MLP_TF = 512

F32 = jnp.float32
BF16 = jnp.bfloat16


def _lambda_init(layer_idx):
    return 0.8 - 0.6 * math.exp(-0.3 * layer_idx)


def _params(semantics):
    return pltpu.CompilerParams(dimension_semantics=semantics,
                                vmem_limit_bytes=VMEM_LIMIT_BYTES_V7X)


def _rms(x, g):
    ms = jnp.mean(x * x, axis=-1, keepdims=True)
    return x * lax.rsqrt(ms + RMS_EPS) * g


def _gelu(x):
    return 0.5 * x * (1.0 + lax.erf(x * SQRT_HALF))


def _in_proj_kernel(x_ref, g_ref, w_ref, rcq_ref, rsq_ref, rck_ref, rsk_ref, lng_ref, lnb_ref,
                    sw_ref, sbt_ref, qt_ref, vt_ref, z_ref, h_ref, su_ref, svn_ref):
    tm = x_ref.shape[0]
    h_ref[...] = _rms(x_ref[...], g_ref[...]).astype(BF16)

    def proj(c):
        return jnp.dot(h_ref[...], w_ref[:, c * ATTN_W:(c + 1) * ATTN_W],
                       preferred_element_type=F32)

    def rotary(acc, rc, rs, u):
        t = acc[:, u * DH:(u + 1) * DH]
        return t * rc + pltpu.roll(t, DH // 2, 1) * rs

    acc = proj(0)
    rc, rs = rcq_ref[...], rsq_ref[...]
    for u in range(ATTN_W // DH):
        qt_ref[u * DH:(u + 1) * DH, :] = rotary(acc, rc, rs, u).T.astype(BF16)

    acc = proj(1)
    rc, rs = rck_ref[...], rsk_ref[...]
    for u in range(ATTN_W // DH):
        z_ref[:, u * DH:(u + 1) * DH] = rotary(acc, rc, rs, u).astype(BF16)

    vt = proj(2).T.astype(BF16)
    for hd in range(N_HEADS):
        vt_ref[hd * VT_ROWS:hd * VT_ROWS + HEAD_W, :] = vt[hd * HEAD_W:(hd + 1) * HEAD_W, :]
        vt_ref[hd * VT_ROWS + HEAD_W:(hd + 1) * VT_ROWS, :] = jnp.ones((ONES_ROWS, tm), BF16)

    su_ref[...] = _gelu(proj(3))
    y = _gelu(proj(4))
    mu = jnp.mean(y, axis=-1, keepdims=True)
    yc = y - mu
    var = jnp.mean(yc * yc, axis=-1, keepdims=True)
    svn_ref[...] = (yc * lax.rsqrt(var + LN_EPS) * lng_ref[...] + lnb_ref[...]).astype(BF16)

    for r in range(tm // CHUNK):
        rows = slice(r * CHUNK, (r + 1) * CHUNK)
        for g in range(SGU_GROUPS):
            cols = slice(g * CHUNK, (g + 1) * CHUNK)
            mixed = jnp.dot(sw_ref[g], svn_ref[rows, cols], preferred_element_type=F32)
            mixed = mixed + sbt_ref[:, g:g + 1]
            out_cols = slice(ATTN_W + g * CHUNK, ATTN_W + (g + 1) * CHUNK)
            z_ref[rows, out_cols] = (su_ref[rows, cols] * mixed).astype(BF16)


def _in_proj(x2, batch, seq_len, norm_g, w_proj, rot, ln_g, ln_b, sgu_w_bf, sgu_bt, tm):
    T = x2.shape[0]
    n_pos_blocks = seq_len // tm
    rot_spec = pl.BlockSpec((tm, DH), lambda i: (i % n_pos_blocks, 0))
    whole = lambda a: pl.BlockSpec(a.shape, lambda i: (0,) * a.ndim)
    t_spec = lambda rows: pl.BlockSpec((None, None, rows, tm),
                                       lambda i: (i // n_pos_blocks, i % n_pos_blocks, 0, 0))
    t_shape = lambda rows: jax.ShapeDtypeStruct((batch, n_pos_blocks, rows, tm), BF16)
    return pl.pallas_call(
        _in_proj_kernel,
        grid=(T // tm,),
        in_specs=[
            pl.BlockSpec((tm, D_MODEL), lambda i: (i, 0)),
            whole(norm_g), whole(w_proj),
            rot_spec, rot_spec, rot_spec, rot_spec,
            whole(ln_g), whole(ln_b), whole(sgu_w_bf), whole(sgu_bt),
        ],
        out_specs=[t_spec(ATTN_W), t_spec(N_HEADS * VT_ROWS),
                   pl.BlockSpec((tm, ZCOLS), lambda i: (i, 0))],
        out_shape=[t_shape(ATTN_W), t_shape(N_HEADS * VT_ROWS),
                   jax.ShapeDtypeStruct((T, ZCOLS), BF16)],
        scratch_shapes=[pltpu.VMEM((tm, D_MODEL), BF16),
                        pltpu.VMEM((tm, SGU_W), F32),
                        pltpu.VMEM((tm, SGU_W), BF16)],
        compiler_params=_params(("parallel",)),
    )(x2, norm_g, w_proj, rot[0], rot[1], rot[2], rot[3], ln_g, ln_b, sgu_w_bf, sgu_bt)


def _attn_kernel(qt_ref, k_ref, vt_ref, lq1_ref, lk1_ref, lq2_ref, lk2_ref, g_ref, o_ref,
                 s0_ref, s1_ref, p0_ref, p1_ref, acc_ref, *, tk, n_kv, lambda_init):
    s_refs, p_refs = (s0_ref, s1_ref), (p0_ref, p1_ref)

    def scores(j, slot):
        rows = pl.ds(pl.multiple_of(j * tk, tk), tk)
        col_max = []
        for c in range(2):
            s = jnp.dot(k_ref[rows, c * DH:(c + 1) * DH], qt_ref[c * DH:(c + 1) * DH, :],
                        preferred_element_type=F32)
            s_refs[slot][c] = s
            col_max.append(jnp.max(s, axis=0, keepdims=True))
        return tuple(col_max)

    def softmax(slot, col_max, run_max):
        new_max, alphas = [], []
        for c in range(2):
            m_new = jnp.maximum(run_max[c], col_max[c])
            alphas.append(jnp.exp2(run_max[c] - m_new))
            p_refs[slot][c] = jnp.exp2(s_refs[slot][c] - m_new).astype(BF16)
            new_max.append(m_new)
        return tuple(new_max), tuple(alphas)

    def values(j, slot, alphas):
        vt = vt_ref[j]
        for c in range(2):
            pv = jnp.dot(vt, p_refs[slot][c], preferred_element_type=F32)
            acc_ref[c] = alphas[c] * acc_ref[c] + pv

    tq = qt_ref.shape[1]
    acc_ref[...] = jnp.zeros(acc_ref.shape, F32)
    run_max = tuple(jnp.full((1, tq), -jnp.inf, F32) for _ in range(2))

    col_max = scores(0, 0)
    col_max_next = scores(1, 1)
    run_max, alphas = softmax(0, col_max, run_max)

    def two_chunks(u, carry):
        col_max, run_max, alphas = carry
        col_max_a = scores(2 * u + 2, 0)
        run_max, alphas_a = softmax(1, col_max, run_max)
        values(2 * u, 0, alphas)
        col_max_b = scores(2 * u + 3, 1)
        run_max, alphas_b = softmax(0, col_max_a, run_max)
        values(2 * u + 1, 1, alphas_a)
        return col_max_b, run_max, alphas_b

    col_max, run_max, alphas = lax.fori_loop(0, n_kv // 2 - 1, two_chunks,
                                             (col_max_next, run_max, alphas))
    _, alphas_last = softmax(1, col_max, run_max)
    values(n_kv - 2, 0, alphas)
    values(n_kv - 1, 1, alphas_last)

    l1, l2 = acc_ref[0, HEAD_W:HEAD_W + 1, :], acc_ref[1, HEAD_W:HEAD_W + 1, :]
    lam = (jnp.exp(jnp.sum(lq1_ref[...] * lk1_ref[...], axis=-1, keepdims=True))
           - jnp.exp(jnp.sum(lq2_ref[...] * lk2_ref[...], axis=-1, keepdims=True))
           + lambda_init)
    o_t = (acc_ref[0, :HEAD_W, :] * (1.0 / l1)
           - lam * (acc_ref[1, :HEAD_W, :] * (1.0 / l2)))
    ms = jnp.mean(o_t * o_t, axis=0, keepdims=True)
    y = (o_t * lax.rsqrt(ms + RMS_EPS)).T
    o_ref[...] = (y * g_ref[...] * (1.0 - lambda_init)).astype(BF16)


def _attention(qt4, z3, vt4, lq1, lk1, lq2, lk2, subln_g, lambda_init):
    B, n_kv, _, tk = vt4.shape
    n_q, tq = qt4.shape[1], qt4.shape[3]
    S = z3.shape[1]
    assert n_kv >= 2 and n_kv % 2 == 0, "the chunk pipeline takes an even number of key chunks"
    vec = lambda n: pl.BlockSpec((1, n), lambda b, h, i: (0, 0))
    kern = functools.partial(_attn_kernel, tk=tk, n_kv=n_kv, lambda_init=lambda_init)
    slot = lambda dt: pltpu.VMEM((2, tk, tq), dt)
    return pl.pallas_call(
        kern,
        grid=(B, N_HEADS, n_q),
        in_specs=[
            pl.BlockSpec((None, None, HEAD_W, tq), lambda b, h, i: (b, i, h, 0)),
            pl.BlockSpec((None, S, HEAD_W), lambda b, h, i: (b, 0, h)),
            pl.BlockSpec((None, n_kv, VT_ROWS, tk), lambda b, h, i: (b, 0, h, 0)),
            vec(DH), vec(DH), vec(DH), vec(DH),
            vec(HEAD_W),
        ],
        out_specs=pl.BlockSpec((None, tq, HEAD_W), lambda b, h, i: (b, i, h)),
        out_shape=jax.ShapeDtypeStruct((B, S, ATTN_W), BF16),
        scratch_shapes=[slot(F32), slot(F32), slot(BF16), slot(BF16),
                        pltpu.VMEM((2, VT_ROWS, tq), F32)],
        compiler_params=_params(("parallel", "parallel", "arbitrary")),
    )(qt4, z3, vt4, lq1, lk1, lq2, lk2, subln_g)


def _merge_kernel(x_ref, osgu_ref, oa_ref, g_ref, wga_ref, wgs_ref, wba_ref, wbs_ref, wo_ref,
                  y_ref, h_ref):
    @pl.when(pl.program_id(1) == 0)
    def _():
        x = x_ref[...]
        y_ref[...] = x
        h_ref[...] = _rms(x, g_ref[...]).astype(BF16)

    h = h_ref[...]
    ga = jax.nn.sigmoid(jnp.dot(h, wga_ref[...], preferred_element_type=F32))
    gs = jax.nn.sigmoid(jnp.dot(h, wgs_ref[...], preferred_element_type=F32))
    ba = jnp.dot(oa_ref[...], wba_ref[...], preferred_element_type=F32)
    bs = jnp.dot(osgu_ref[...], wbs_ref[...], preferred_element_type=F32)
    merged = (ga * ba + gs * bs).astype(BF16)
    y_ref[...] += jnp.dot(merged, wo_ref[...], preferred_element_type=F32)


def _merge(x2, z, oa, norm_g, w_gates, wba, wbs, wo, tm, tn):
    T = x2.shape[0]
    return pl.pallas_call(
        _merge_kernel,
        grid=(T // tm, D_MODEL // tn),
        in_specs=[
            pl.BlockSpec((tm, D_MODEL), lambda i, j: (i, 0)),
            pl.BlockSpec((tm, SGU_W), lambda i, j: (i, ATTN_W // SGU_W)),
            pl.BlockSpec((tm, ATTN_W), lambda i, j: (i, 0)),
            pl.BlockSpec((1, D_MODEL), lambda i, j: (0, 0)),
            pl.BlockSpec((D_MODEL, tn), lambda i, j: (0, j)),
            pl.BlockSpec((D_MODEL, tn), lambda i, j: (0, D_MODEL // tn + j)),
            pl.BlockSpec((ATTN_W, tn), lambda i, j: (0, j)),
            pl.BlockSpec((SGU_W, tn), lambda i, j: (0, j)),
            pl.BlockSpec((tn, D_MODEL), lambda i, j: (j, 0)),
        ],
        out_specs=pl.BlockSpec((tm, D_MODEL), lambda i, j: (i, 0)),
        out_shape=jax.ShapeDtypeStruct((T, D_MODEL), F32),
        scratch_shapes=[pltpu.VMEM((tm, D_MODEL), BF16)],
        compiler_params=_params(("parallel", "arbitrary")),
    )(x2, z, oa, norm_g, w_gates, w_gates, wba, wbs, wo)


def _mlp_kernel(x_ref, g_ref, wu_ref, wd_ref, fg_ref, y_ref, h_ref, *, final_norm):
    j = pl.program_id(1)

    @pl.when(j == 0)
    def _():
        x = x_ref[...]
        y_ref[...] = x
        h_ref[...] = _rms(x, g_ref[...]).astype(BF16)

    u = jnp.dot(h_ref[...], wu_ref[...], preferred_element_type=F32)
    u = jnp.square(jnp.maximum(u, 0.0)).astype(BF16)
    y_ref[...] += jnp.dot(u, wd_ref[...], preferred_element_type=F32)

    if final_norm:
        @pl.when(j == pl.num_programs(1) - 1)
        def _():
            y_ref[...] = _rms(y_ref[...], fg_ref[...])


def _mlp(x2, norm_g, wu, wd, final_g, final_norm, tm, tf):
    T = x2.shape[0]
    return pl.pallas_call(
        functools.partial(_mlp_kernel, final_norm=final_norm),
        grid=(T // tm, D_FF // tf),
        in_specs=[
            pl.BlockSpec((tm, D_MODEL), lambda i, j: (i, 0)),
            pl.BlockSpec((1, D_MODEL), lambda i, j: (0, 0)),
            pl.BlockSpec((D_MODEL, tf), lambda i, j: (0, j)),
            pl.BlockSpec((tf, D_MODEL), lambda i, j: (j, 0)),
            pl.BlockSpec((1, D_MODEL), lambda i, j: (0, 0)),
        ],
        out_specs=pl.BlockSpec((tm, D_MODEL), lambda i, j: (i, 0)),
        out_shape=jax.ShapeDtypeStruct((T, D_MODEL), F32),
        scratch_shapes=[pltpu.VMEM((tm, D_MODEL), BF16)],
        compiler_params=_params(("parallel", "arbitrary")),
    )(x2, norm_g, wu, wd, final_g)


def _rotary_lane_order(w):
    split = ROT_DIM + DH // 2 - ROT_HALF
    return jnp.concatenate([w[..., :ROT_HALF], w[..., ROT_DIM:split],
                            w[..., ROT_HALF:ROT_DIM], w[..., split:]], axis=-1)


def _rot_tables(seq_len):
    pos = jnp.arange(seq_len, dtype=F32)
    inv_freq = 1.0 / (jnp.float32(ROPE_THETA) ** (jnp.arange(0, ROT_DIM, 2, dtype=F32) / ROT_DIM))
    ang = pos[:, None] * inv_freq[None, :]
    cos, sin = jnp.cos(ang), jnp.sin(ang)
    pad = DH // 2 - ROT_HALF
    ones, zeros = jnp.ones((seq_len, pad), F32), jnp.zeros((seq_len, pad), F32)
    rc = jnp.concatenate([cos, ones, cos, ones], axis=-1)
    rs = jnp.concatenate([-sin, zeros, sin, zeros], axis=-1)
    return rc * EXP2_SCALE, rs * EXP2_SCALE, rc, rs


def _token_tile(seq_len):
    return min(512, seq_len // 2)


def _prepare(p):
    row = lambda a: a.reshape(1, -1).astype(F32)
    layers = []
    for l in range(p["w_in"].shape[0]):
        w_in = p["w_in"][l]
        w_qk = _rotary_lane_order(w_in[:, :2 * ATTN_W].reshape(D_MODEL, -1, DH))
        w_proj = jnp.concatenate([w_qk.reshape(D_MODEL, 2 * ATTN_W),
                                  w_in[:, 2 * ATTN_W:PROJ_COLS]], axis=1).astype(BF16)
        layers.append(dict(
            attn_norm_g=row(p["attn_norm_g"][l]), w_proj=w_proj,
            w_gates=w_in[:, PROJ_COLS:].astype(BF16),
            lq1=row(p["lambda_q1"][l]), lk1=row(p["lambda_k1"][l]),
            lq2=row(p["lambda_q2"][l]), lk2=row(p["lambda_k2"][l]),
            subln_g=row(p["subln_g"][l]), ln_g=row(p["sgu_ln_g"][l]), ln_b=row(p["sgu_ln_b"][l]),
            sgu_w=p["sgu_w"][l].astype(BF16), sgu_bt=p["sgu_b"][l].T.astype(F32),
            w_br_attn=p["w_br_attn"][l].astype(BF16), w_br_sgu=p["w_br_sgu"][l].astype(BF16),
            w_out=p["w_out"][l].astype(BF16), mlp_norm_g=row(p["mlp_norm_g"][l]),
            w_up=p["w_up"][l].astype(BF16), w_down=p["w_down"][l].astype(BF16)))
    return layers, row(p["final_norm_g"])


def _trunk(x, layers, final_g):
    B, S, _ = x.shape
    tm = _token_tile(S)
    rot = _rot_tables(S)
    x2 = x.reshape(B * S, D_MODEL)
    for l, w in enumerate(layers):
        qt4, vt4, z = _in_proj(x2, B, S, w["attn_norm_g"], w["w_proj"], rot, w["ln_g"], w["ln_b"],
                               w["sgu_w"], w["sgu_bt"], tm)
        oa = _attention(qt4, z.reshape(B, S, ZCOLS), vt4, w["lq1"], w["lk1"], w["lq2"], w["lk2"],
                        w["subln_g"], _lambda_init(l))
        x2 = _merge(x2, z, oa.reshape(B * S, ATTN_W), w["attn_norm_g"], w["w_gates"],
                    w["w_br_attn"], w["w_br_sgu"], w["w_out"], WIDE_TM, MERGE_TN)
        x2 = _mlp(x2, w["mlp_norm_g"], w["w_up"], w["w_down"], final_g,
                  l == len(layers) - 1, WIDE_TM, MLP_TF)
    return x2.reshape(B, S, D_MODEL)


def kernel(x_prompt, x_sample, attn_norm_g, w_in, lambda_q1, lambda_k1, lambda_q2, lambda_k2,
           subln_g, sgu_ln_g, sgu_ln_b, sgu_w, sgu_b, w_br_attn, w_br_sgu, w_out, mlp_norm_g,
           w_up, w_down, final_norm_g):
    p = dict(attn_norm_g=attn_norm_g, w_in=w_in, lambda_q1=lambda_q1, lambda_k1=lambda_k1,
             lambda_q2=lambda_q2, lambda_k2=lambda_k2, subln_g=subln_g, sgu_ln_g=sgu_ln_g,
             sgu_ln_b=sgu_ln_b, sgu_w=sgu_w, sgu_b=sgu_b, w_br_attn=w_br_attn, w_br_sgu=w_br_sgu,
             w_out=w_out, mlp_norm_g=mlp_norm_g, w_up=w_up, w_down=w_down,
             final_norm_g=final_norm_g)
    layers, final_g = _prepare(p)
    return _trunk(x_prompt, layers, final_g), _trunk(x_sample, layers, final_g)
```

```python
import functools
import math

import jax
import jax.numpy as jnp
from jax import lax
from jax.experimental import pallas as pl
from jax.experimental.pallas import tpu as pltpu

D_MODEL = 2048
N_HEADS = 4
DH = 128
HEAD_W = 2 * DH
ATTN_W = N_HEADS * HEAD_W
ONES_ROWS = 16
VT_ROWS = HEAD_W + ONES_ROWS
ROT_DIM = DH // 4
ROT_HALF = ROT_DIM // 2
ROPE_THETA = 500000.0
ATTN_SCALE = DH ** -0.5
CHUNK = 128
SGU_GROUPS = 8
SGU_W = SGU_GROUPS * CHUNK
D_FF = 4 * D_MODEL
PROJ_COLS = 3 * ATTN_W + 2 * SGU_W
ZCOLS = ATTN_W + SGU_W
RMS_EPS = 1e-6
LN_EPS = 1e-5
SQRT_HALF = math.sqrt(0.5)
EXP2_SCALE = ATTN_SCALE * math.log2(math.e)

VMEM_LIMIT_BYTES_V7X = 56 * 1024 * 1024

MERGE_TN = 512
MLP_TF = 1024

F32 = jnp.float32
BF16 = jnp.bfloat16


def _lambda_init(layer_idx):
    return 0.8 - 0.6 * math.exp(-0.3 * layer_idx)


def _params(semantics):
    return pltpu.CompilerParams(dimension_semantics=semantics,
                                vmem_limit_bytes=VMEM_LIMIT_BYTES_V7X)


def _rms(x, g):
    ms = jnp.mean(x * x, axis=-1, keepdims=True)
    return x * lax.rsqrt(ms + RMS_EPS) * g


def _gelu(x):
    return 0.5 * x * (1.0 + lax.erf(x * SQRT_HALF))


def _in_proj_kernel(x_ref, g_ref, w_ref, rcq_ref, rsq_ref, rck_ref, rsk_ref, lng_ref, lnb_ref,
                    sw_ref, sbt_ref, qt_ref, vt_ref, z_ref, h_ref, su_ref, svn_ref):
    tm = x_ref.shape[0]
    h_ref[...] = _rms(x_ref[...], g_ref[...]).astype(BF16)

    def proj(c):
        return jnp.dot(h_ref[...], w_ref[:, c * ATTN_W:(c + 1) * ATTN_W],
                       preferred_element_type=F32)

    def rotary(acc, rc, rs, u):
        t = acc[:, u * DH:(u + 1) * DH]
        return t * rc + pltpu.roll(t, DH // 2, 1) * rs

    acc = proj(0)
    rc, rs = rcq_ref[...], rsq_ref[...]
    for u in range(ATTN_W // DH):
        qt_ref[u * DH:(u + 1) * DH, :] = rotary(acc, rc, rs, u).T.astype(BF16)

    acc = proj(1)
    rc, rs = rck_ref[...], rsk_ref[...]
    for u in range(ATTN_W // DH):
        z_ref[:, u * DH:(u + 1) * DH] = rotary(acc, rc, rs, u).astype(BF16)

    vt = proj(2).T.astype(BF16)
    for hd in range(N_HEADS):
        vt_ref[hd * VT_ROWS:hd * VT_ROWS + HEAD_W, :] = vt[hd * HEAD_W:(hd + 1) * HEAD_W, :]
        vt_ref[hd * VT_ROWS + HEAD_W:(hd + 1) * VT_ROWS, :] = jnp.ones((ONES_ROWS, tm), BF16)

    su_ref[...] = _gelu(proj(3))
    y = _gelu(proj(4))
    mu = jnp.mean(y, axis=-1, keepdims=True)
    yc = y - mu
    var = jnp.mean(yc * yc, axis=-1, keepdims=True)
    svn_ref[...] = (yc * lax.rsqrt(var + LN_EPS) * lng_ref[...] + lnb_ref[...]).astype(BF16)

    for r in range(tm // CHUNK):
        rows = slice(r * CHUNK, (r + 1) * CHUNK)
        for g in range(SGU_GROUPS):
            cols = slice(g * CHUNK, (g + 1) * CHUNK)
            mixed = jnp.dot(sw_ref[g], svn_ref[rows, cols], preferred_element_type=F32)
            mixed = mixed + sbt_ref[:, g:g + 1]
            out_cols = slice(ATTN_W + g * CHUNK, ATTN_W + (g + 1) * CHUNK)
            z_ref[rows, out_cols] = (su_ref[rows, cols] * mixed).astype(BF16)


def _in_proj(x2, batch, seq_len, norm_g, w_proj, rot, ln_g, ln_b, sgu_w_bf, sgu_bt, tm):
    T = x2.shape[0]
    n_pos_blocks = seq_len // tm
    rot_spec = pl.BlockSpec((tm, DH), lambda i: (i % n_pos_blocks, 0))
    whole = lambda a: pl.BlockSpec(a.shape, lambda i: (0,) * a.ndim)
    t_spec = lambda rows: pl.BlockSpec((None, None, rows, tm),
                                       lambda i: (i // n_pos_blocks, i % n_pos_blocks, 0, 0))
    t_shape = lambda rows: jax.ShapeDtypeStruct((batch, n_pos_blocks, rows, tm), BF16)
    return pl.pallas_call(
        _in_proj_kernel,
        grid=(T // tm,),
        in_specs=[
            pl.BlockSpec((tm, D_MODEL), lambda i: (i, 0)),
            whole(norm_g), whole(w_proj),
            rot_spec, rot_spec, rot_spec, rot_spec,
            whole(ln_g), whole(ln_b), whole(sgu_w_bf), whole(sgu_bt),
        ],
        out_specs=[t_spec(ATTN_W), t_spec(N_HEADS * VT_ROWS),
                   pl.BlockSpec((tm, ZCOLS), lambda i: (i, 0))],
        out_shape=[t_shape(ATTN_W), t_shape(N_HEADS * VT_ROWS),
                   jax.ShapeDtypeStruct((T, ZCOLS), BF16)],
        scratch_shapes=[pltpu.VMEM((tm, D_MODEL), BF16),
                        pltpu.VMEM((tm, SGU_W), F32),
                        pltpu.VMEM((tm, SGU_W), BF16)],
        compiler_params=_params(("parallel",)),
    )(x2, norm_g, w_proj, rot[0], rot[1], rot[2], rot[3], ln_g, ln_b, sgu_w_bf, sgu_bt)


def _attn_kernel(qt_ref, k_ref, vt_ref, lq1_ref, lk1_ref, lq2_ref, lk2_ref, g_ref, o_ref,
                 s0_ref, s1_ref, p0_ref, p1_ref, acc_ref, *, tk, n_kv, lambda_init):
    s_refs, p_refs = (s0_ref, s1_ref), (p0_ref, p1_ref)

    def scores(j, slot):
        rows = pl.ds(pl.multiple_of(j * tk, tk), tk)
        col_max = []
        for c in range(2):
            s = jnp.dot(k_ref[rows, c * DH:(c + 1) * DH], qt_ref[c * DH:(c + 1) * DH, :],
                        preferred_element_type=F32)
            s_refs[slot][c] = s
            col_max.append(jnp.max(s, axis=0, keepdims=True))
        return tuple(col_max)

    def softmax(slot, col_max, run_max):
        new_max, alphas = [], []
        for c in range(2):
            m_new = jnp.maximum(run_max[c], col_max[c])
            alphas.append(jnp.exp2(run_max[c] - m_new))
            p_refs[slot][c] = jnp.exp2(s_refs[slot][c] - m_new).astype(BF16)
            new_max.append(m_new)
        return tuple(new_max), tuple(alphas)

    def values(j, slot, alphas):
        vt = vt_ref[j]
        for c in range(2):
            pv = jnp.dot(vt, p_refs[slot][c], preferred_element_type=F32)
            acc_ref[c] = alphas[c] * acc_ref[c] + pv

    tq = qt_ref.shape[1]
    acc_ref[...] = jnp.zeros(acc_ref.shape, F32)
    run_max = tuple(jnp.full((1, tq), -jnp.inf, F32) for _ in range(2))

    col_max = scores(0, 0)
    col_max_next = scores(1, 1)
    run_max, alphas = softmax(0, col_max, run_max)

    def two_chunks(u, carry):
        col_max, run_max, alphas = carry
        col_max_a = scores(2 * u + 2, 0)
        run_max, alphas_a = softmax(1, col_max, run_max)
        values(2 * u, 0, alphas)
        col_max_b = scores(2 * u + 3, 1)
        run_max, alphas_b = softmax(0, col_max_a, run_max)
        values(2 * u + 1, 1, alphas_a)
        return col_max_b, run_max, alphas_b

    col_max, run_max, alphas = lax.fori_loop(0, n_kv // 2 - 1, two_chunks,
                                             (col_max_next, run_max, alphas))
    _, alphas_last = softmax(1, col_max, run_max)
    values(n_kv - 2, 0, alphas)
    values(n_kv - 1, 1, alphas_last)

    l1, l2 = acc_ref[0, HEAD_W:HEAD_W + 1, :], acc_ref[1, HEAD_W:HEAD_W + 1, :]
    lam = (jnp.exp(jnp.sum(lq1_ref[...] * lk1_ref[...], axis=-1, keepdims=True))
           - jnp.exp(jnp.sum(lq2_ref[...] * lk2_ref[...], axis=-1, keepdims=True))
           + lambda_init)
    o_t = (acc_ref[0, :HEAD_W, :] * (1.0 / l1)
           - lam * (acc_ref[1, :HEAD_W, :] * (1.0 / l2)))
    ms = jnp.mean(o_t * o_t, axis=0, keepdims=True)
    y = (o_t * lax.rsqrt(ms + RMS_EPS)).T
    o_ref[...] = (y * g_ref[...] * (1.0 - lambda_init)).astype(BF16)


def _attention(qt4, z3, vt4, lq1, lk1, lq2, lk2, subln_g, lambda_init):
    B, n_kv, _, tk = vt4.shape
    n_q, tq = qt4.shape[1], qt4.shape[3]
    S = z3.shape[1]
    assert n_kv >= 2 and n_kv % 2 == 0, "the chunk pipeline takes an even number of key chunks"
    vec = lambda n: pl.BlockSpec((1, n), lambda b, h, i: (0, 0))
    kern = functools.partial(_attn_kernel, tk=tk, n_kv=n_kv, lambda_init=lambda_init)
    slot = lambda dt: pltpu.VMEM((2, tk, tq), dt)
    return pl.pallas_call(
        kern,
        grid=(B, N_HEADS, n_q),
        in_specs=[
            pl.BlockSpec((None, None, HEAD_W, tq), lambda b, h, i: (b, i, h, 0)),
            pl.BlockSpec((None, S, HEAD_W), lambda b, h, i: (b, 0, h)),
            pl.BlockSpec((None, n_kv, VT_ROWS, tk), lambda b, h, i: (b, 0, h, 0)),
            vec(DH), vec(DH), vec(DH), vec(DH),
            vec(HEAD_W),
        ],
        out_specs=pl.BlockSpec((None, tq, HEAD_W), lambda b, h, i: (b, i, h)),
        out_shape=jax.ShapeDtypeStruct((B, S, ATTN_W), BF16),
        scratch_shapes=[slot(F32), slot(F32), slot(BF16), slot(BF16),
                        pltpu.VMEM((2, VT_ROWS, tq), F32)],
        compiler_params=_params(("parallel", "parallel", "arbitrary")),
    )(qt4, z3, vt4, lq1, lk1, lq2, lk2, subln_g)


def _merge_kernel(x_ref, osgu_ref, oa_ref, g_ref, wga_ref, wgs_ref, wba_ref, wbs_ref, wo_ref,
                  y_ref, h_ref):
    @pl.when(pl.program_id(1) == 0)
    def _():
        x = x_ref[...]
        y_ref[...] = x
        h_ref[...] = _rms(x, g_ref[...]).astype(BF16)

    h = h_ref[...]
    ga = jax.nn.sigmoid(jnp.dot(h, wga_ref[...], preferred_element_type=F32))
    gs = jax.nn.sigmoid(jnp.dot(h, wgs_ref[...], preferred_element_type=F32))
    ba = jnp.dot(oa_ref[...], wba_ref[...], preferred_element_type=F32)
    bs = jnp.dot(osgu_ref[...], wbs_ref[...], preferred_element_type=F32)
    merged = (ga * ba + gs * bs).astype(BF16)
    y_ref[...] += jnp.dot(merged, wo_ref[...], preferred_element_type=F32)


def _merge(x2, z, oa, norm_g, w_gates, wba, wbs, wo, tm, tn):
    T = x2.shape[0]
    return pl.pallas_call(
        _merge_kernel,
        grid=(T // tm, D_MODEL // tn),
        in_specs=[
            pl.BlockSpec((tm, D_MODEL), lambda i, j: (i, 0)),
            pl.BlockSpec((tm, SGU_W), lambda i, j: (i, ATTN_W // SGU_W)),
            pl.BlockSpec((tm, ATTN_W), lambda i, j: (i, 0)),
            pl.BlockSpec((1, D_MODEL), lambda i, j: (0, 0)),
            pl.BlockSpec((None, D_MODEL, tn), lambda i, j: (j, 0, 0)),
            pl.BlockSpec((None, D_MODEL, tn), lambda i, j: (D_MODEL // tn + j, 0, 0)),
            pl.BlockSpec((None, ATTN_W, tn), lambda i, j: (j, 0, 0)),
            pl.BlockSpec((None, SGU_W, tn), lambda i, j: (j, 0, 0)),
            pl.BlockSpec((tn, D_MODEL), lambda i, j: (j, 0)),
        ],
        out_specs=pl.BlockSpec((tm, D_MODEL), lambda i, j: (i, 0)),
        out_shape=jax.ShapeDtypeStruct((T, D_MODEL), F32),
        scratch_shapes=[pltpu.VMEM((tm, D_MODEL), BF16)],
        compiler_params=_params(("parallel", "arbitrary")),
    )(x2, z, oa, norm_g, w_gates, w_gates, wba, wbs, wo)


def _mlp_kernel(x_ref, g_ref, wu_ref, wd_ref, fg_ref, y_ref, h_ref, *, final_norm):
    j = pl.program_id(1)

    @pl.when(j == 0)
    def _():
        x = x_ref[...]
        y_ref[...] = x
        h_ref[...] = _rms(x, g_ref[...]).astype(BF16)

    u = jnp.dot(h_ref[...], wu_ref[...], preferred_element_type=F32)
    u = jnp.square(jnp.maximum(u, 0.0)).astype(BF16)
    y_ref[...] += jnp.dot(u, wd_ref[...], preferred_element_type=F32)

    if final_norm:
        @pl.when(j == pl.num_programs(1) - 1)
        def _():
            y_ref[...] = _rms(y_ref[...], fg_ref[...])


def _mlp(x2, norm_g, wu, wd, final_g, final_norm, tm, tf):
    T = x2.shape[0]
    return pl.pallas_call(
        functools.partial(_mlp_kernel, final_norm=final_norm),
        grid=(T // tm, D_FF // tf),
        in_specs=[
            pl.BlockSpec((tm, D_MODEL), lambda i, j: (i, 0)),
            pl.BlockSpec((1, D_MODEL), lambda i, j: (0, 0)),
            pl.BlockSpec((None, D_MODEL, tf), lambda i, j: (j, 0, 0)),
            pl.BlockSpec((tf, D_MODEL), lambda i, j: (j, 0)),
            pl.BlockSpec((1, D_MODEL), lambda i, j: (0, 0)),
        ],
        out_specs=pl.BlockSpec((tm, D_MODEL), lambda i, j: (i, 0)),
        out_shape=jax.ShapeDtypeStruct((T, D_MODEL), F32),
        scratch_shapes=[pltpu.VMEM((tm, D_MODEL), BF16)],
        compiler_params=_params(("parallel", "arbitrary")),
    )(x2, norm_g, wu, wd, final_g)


def _rotary_lane_order(w):
    split = ROT_DIM + DH // 2 - ROT_HALF
    return jnp.concatenate([w[..., :ROT_HALF], w[..., ROT_DIM:split],
                            w[..., ROT_HALF:ROT_DIM], w[..., split:]], axis=-1)


def _rot_tables(seq_len):
    pos = jnp.arange(seq_len, dtype=F32)
    inv_freq = 1.0 / (jnp.float32(ROPE_THETA) ** (jnp.arange(0, ROT_DIM, 2, dtype=F32) / ROT_DIM))
    ang = pos[:, None] * inv_freq[None, :]
    cos, sin = jnp.cos(ang), jnp.sin(ang)
    pad = DH // 2 - ROT_HALF
    ones, zeros = jnp.ones((seq_len, pad), F32), jnp.zeros((seq_len, pad), F32)
    rc = jnp.concatenate([cos, ones, cos, ones], axis=-1)
    rs = jnp.concatenate([-sin, zeros, sin, zeros], axis=-1)
    return rc * EXP2_SCALE, rs * EXP2_SCALE, rc, rs


def _token_tile(seq_len):
    return min(512, seq_len // 2)


def _column_blocks(w, width):
    rows, cols = w.shape
    return w.astype(BF16).reshape(rows, cols // width, width).transpose(1, 0, 2)


def _prepare(p):
    row = lambda a: a.reshape(1, -1).astype(F32)
    layers = []
    for l in range(p["w_in"].shape[0]):
        w_in = p["w_in"][l]
        w_qk = _rotary_lane_order(w_in[:, :2 * ATTN_W].reshape(D_MODEL, -1, DH))
        w_proj = jnp.concatenate([w_qk.reshape(D_MODEL, 2 * ATTN_W),
                                  w_in[:, 2 * ATTN_W:PROJ_COLS]], axis=1).astype(BF16)
        layers.append(dict(
            attn_norm_g=row(p["attn_norm_g"][l]), w_proj=w_proj,
            w_gates=_column_blocks(w_in[:, PROJ_COLS:], MERGE_TN),
            lq1=row(p["lambda_q1"][l]), lk1=row(p["lambda_k1"][l]),
            lq2=row(p["lambda_q2"][l]), lk2=row(p["lambda_k2"][l]),
            subln_g=row(p["subln_g"][l]), ln_g=row(p["sgu_ln_g"][l]), ln_b=row(p["sgu_ln_b"][l]),
            sgu_w=p["sgu_w"][l].astype(BF16), sgu_bt=p["sgu_b"][l].T.astype(F32),
            w_br_attn=_column_blocks(p["w_br_attn"][l], MERGE_TN),
            w_br_sgu=_column_blocks(p["w_br_sgu"][l], MERGE_TN),
            w_out=p["w_out"][l].astype(BF16), mlp_norm_g=row(p["mlp_norm_g"][l]),
            w_up=_column_blocks(p["w_up"][l], MLP_TF), w_down=p["w_down"][l].astype(BF16)))
    return layers, row(p["final_norm_g"])


def _trunk(x, layers, final_g):
    B, S, _ = x.shape
    tm = _token_tile(S)
    rot = _rot_tables(S)
    x2 = x.reshape(B * S, D_MODEL)
    for l, w in enumerate(layers):
        qt4, vt4, z = _in_proj(x2, B, S, w["attn_norm_g"], w["w_proj"], rot, w["ln_g"], w["ln_b"],
                               w["sgu_w"], w["sgu_bt"], tm)
        oa = _attention(qt4, z.reshape(B, S, ZCOLS), vt4, w["lq1"], w["lk1"], w["lq2"], w["lk2"],
                        w["subln_g"], _lambda_init(l))
        x2 = _merge(x2, z, oa.reshape(B * S, ATTN_W), w["attn_norm_g"], w["w_gates"],
                    w["w_br_attn"], w["w_br_sgu"], w["w_out"], tm, MERGE_TN)
        x2 = _mlp(x2, w["mlp_norm_g"], w["w_up"], w["w_down"], final_g,
                  l == len(layers) - 1, tm, MLP_TF)
    return x2.reshape(B, S, D_MODEL)


def kernel(x_prompt, x_sample, attn_norm_g, w_in, lambda_q1, lambda_k1, lambda_q2, lambda_k2,
           subln_g, sgu_ln_g, sgu_ln_b, sgu_w, sgu_b, w_br_attn, w_br_sgu, w_out, mlp_norm_g,
           w_up, w_down, final_norm_g):
    p = dict(attn_norm_g=attn_norm_g, w_in=w_in, lambda_q1=lambda_q1, lambda_k1=lambda_k1,
             lambda_q2=lambda_q2, lambda_k2=lambda_k2, subln_g=subln_g, sgu_ln_g=sgu_ln_g,
             sgu_ln_b=sgu_ln_b, sgu_w=sgu_w, sgu_b=sgu_b, w_br_attn=w_br_attn, w_br_sgu=w_br_sgu,
             w_out=w_out, mlp_norm_g=mlp_norm_g, w_up=w_up, w_down=w_down,
             final_norm_g=final_norm_g)
    layers, final_g = _prepare(p)
    return _trunk(x_prompt, layers, final_g), _trunk(x_sample, layers, final_g)
```

```python
import functools
import math

import jax
import jax.numpy as jnp
from jax import lax
from jax.experimental import pallas as pl
from jax.experimental.pallas import tpu as pltpu

D_MODEL = 2048
N_HEADS = 4
DH = 128
HEAD_W = 2 * DH
ATTN_W = N_HEADS * HEAD_W
ONES_ROWS = 16
VT_ROWS = HEAD_W + ONES_ROWS
ROT_DIM = DH // 4
ROT_HALF = ROT_DIM // 2
ROPE_THETA = 500000.0
ATTN_SCALE = DH ** -0.5
CHUNK = 128
SGU_GROUPS = 8
SGU_W = SGU_GROUPS * CHUNK
D_FF = 4 * D_MODEL
PROJ_COLS = 3 * ATTN_W + 2 * SGU_W
ZCOLS = ATTN_W + SGU_W
RMS_EPS = 1e-6
LN_EPS = 1e-5
SQRT_HALF = math.sqrt(0.5)
EXP2_SCALE = ATTN_SCALE * math.log2(math.e)

VMEM_LIMIT_BYTES_V7X = 56 * 1024 * 1024

MERGE_TN = 512
MLP_TF = 1024
WEIGHT_SLOTS = 3

F32 = jnp.float32
BF16 = jnp.bfloat16


def _lambda_init(layer_idx):
    return 0.8 - 0.6 * math.exp(-0.3 * layer_idx)


def _params(semantics):
    return pltpu.CompilerParams(dimension_semantics=semantics,
                                vmem_limit_bytes=VMEM_LIMIT_BYTES_V7X)


def _rms(x, g):
    ms = jnp.mean(x * x, axis=-1, keepdims=True)
    return x * lax.rsqrt(ms + RMS_EPS) * g


def _gelu(x):
    return 0.5 * x * (1.0 + lax.erf(x * SQRT_HALF))


def _in_proj_kernel(x_ref, g_ref, w_ref, rcq_ref, rsq_ref, rck_ref, rsk_ref, lng_ref, lnb_ref,
                    sw_ref, sbt_ref, qt_ref, vt_ref, z_ref, h_ref, su_ref, svn_ref):
    tm = x_ref.shape[0]
    h_ref[...] = _rms(x_ref[...], g_ref[...]).astype(BF16)

    def proj(c):
        return jnp.dot(h_ref[...], w_ref[:, c * ATTN_W:(c + 1) * ATTN_W],
                       preferred_element_type=F32)

    def rotary(acc, rc, rs, u):
        t = acc[:, u * DH:(u + 1) * DH]
        return t * rc + pltpu.roll(t, DH // 2, 1) * rs

    acc = proj(0)
    rc, rs = rcq_ref[...], rsq_ref[...]
    for u in range(ATTN_W // DH):
        qt_ref[u * DH:(u + 1) * DH, :] = rotary(acc, rc, rs, u).T.astype(BF16)

    acc = proj(1)
    rc, rs = rck_ref[...], rsk_ref[...]
    for u in range(ATTN_W // DH):
        z_ref[:, u * DH:(u + 1) * DH] = rotary(acc, rc, rs, u).astype(BF16)

    vt = proj(2).T.astype(BF16)
    for hd in range(N_HEADS):
        vt_ref[hd * VT_ROWS:hd * VT_ROWS + HEAD_W, :] = vt[hd * HEAD_W:(hd + 1) * HEAD_W, :]
        vt_ref[hd * VT_ROWS + HEAD_W:(hd + 1) * VT_ROWS, :] = jnp.ones((ONES_ROWS, tm), BF16)

    su_ref[...] = _gelu(proj(3))
    y = _gelu(proj(4))
    mu = jnp.mean(y, axis=-1, keepdims=True)
    yc = y - mu
    var = jnp.mean(yc * yc, axis=-1, keepdims=True)
    svn_ref[...] = (yc * lax.rsqrt(var + LN_EPS) * lng_ref[...] + lnb_ref[...]).astype(BF16)

    for r in range(tm // CHUNK):
        rows = slice(r * CHUNK, (r + 1) * CHUNK)
        for g in range(SGU_GROUPS):
            cols = slice(g * CHUNK, (g + 1) * CHUNK)
            mixed = jnp.dot(sw_ref[g], svn_ref[rows, cols], preferred_element_type=F32)
            mixed = mixed + sbt_ref[:, g:g + 1]
            out_cols = slice(ATTN_W + g * CHUNK, ATTN_W + (g + 1) * CHUNK)
            z_ref[rows, out_cols] = (su_ref[rows, cols] * mixed).astype(BF16)


def _in_proj(x2, batch, seq_len, norm_g, w_proj, rot, ln_g, ln_b, sgu_w_bf, sgu_bt, tm):
    T = x2.shape[0]
    n_pos_blocks = seq_len // tm
    rot_spec = pl.BlockSpec((tm, DH), lambda i: (i % n_pos_blocks, 0))
    whole = lambda a: pl.BlockSpec(a.shape, lambda i: (0,) * a.ndim)
    t_spec = lambda rows: pl.BlockSpec((None, None, rows, tm),
                                       lambda i: (i // n_pos_blocks, i % n_pos_blocks, 0, 0))
    t_shape = lambda rows: jax.ShapeDtypeStruct((batch, n_pos_blocks, rows, tm), BF16)
    return pl.pallas_call(
        _in_proj_kernel,
        grid=(T // tm,),
        in_specs=[
            pl.BlockSpec((tm, D_MODEL), lambda i: (i, 0)),
            whole(norm_g), whole(w_proj),
            rot_spec, rot_spec, rot_spec, rot_spec,
            whole(ln_g), whole(ln_b), whole(sgu_w_bf), whole(sgu_bt),
        ],
        out_specs=[t_spec(ATTN_W), t_spec(N_HEADS * VT_ROWS),
                   pl.BlockSpec((tm, ZCOLS), lambda i: (i, 0))],
        out_shape=[t_shape(ATTN_W), t_shape(N_HEADS * VT_ROWS),
                   jax.ShapeDtypeStruct((T, ZCOLS), BF16)],
        scratch_shapes=[pltpu.VMEM((tm, D_MODEL), BF16),
                        pltpu.VMEM((tm, SGU_W), F32),
                        pltpu.VMEM((tm, SGU_W), BF16)],
        compiler_params=_params(("parallel",)),
    )(x2, norm_g, w_proj, rot[0], rot[1], rot[2], rot[3], ln_g, ln_b, sgu_w_bf, sgu_bt)


def _attn_kernel(qt_ref, k_ref, vt_ref, lq1_ref, lk1_ref, lq2_ref, lk2_ref, g_ref, o_ref,
                 s0_ref, s1_ref, p0_ref, p1_ref, acc_ref, *, tk, n_kv, lambda_init):
    s_refs, p_refs = (s0_ref, s1_ref), (p0_ref, p1_ref)

    def scores(j, slot):
        rows = pl.ds(pl.multiple_of(j * tk, tk), tk)
        col_max = []
        for c in range(2):
            s = jnp.dot(k_ref[rows, c * DH:(c + 1) * DH], qt_ref[c * DH:(c + 1) * DH, :],
                        preferred_element_type=F32)
            s_refs[slot][c] = s
            col_max.append(jnp.max(s, axis=0, keepdims=True))
        return tuple(col_max)

    def softmax(slot, col_max, run_max):
        new_max, alphas = [], []
        for c in range(2):
            m_new = jnp.maximum(run_max[c], col_max[c])
            alphas.append(jnp.exp2(run_max[c] - m_new))
            p_refs[slot][c] = jnp.exp2(s_refs[slot][c] - m_new).astype(BF16)
            new_max.append(m_new)
        return tuple(new_max), tuple(alphas)

    def values(j, slot, alphas):
        vt = vt_ref[j]
        for c in range(2):
            pv = jnp.dot(vt, p_refs[slot][c], preferred_element_type=F32)
            acc_ref[c] = alphas[c] * acc_ref[c] + pv

    tq = qt_ref.shape[1]
    acc_ref[...] = jnp.zeros(acc_ref.shape, F32)
    run_max = tuple(jnp.full((1, tq), -jnp.inf, F32) for _ in range(2))

    col_max = scores(0, 0)
    col_max_next = scores(1, 1)
    run_max, alphas = softmax(0, col_max, run_max)

    def two_chunks(u, carry):
        col_max, run_max, alphas = carry
        col_max_a = scores(2 * u + 2, 0)
        run_max, alphas_a = softmax(1, col_max, run_max)
        values(2 * u, 0, alphas)
        col_max_b = scores(2 * u + 3, 1)
        run_max, alphas_b = softmax(0, col_max_a, run_max)
        values(2 * u + 1, 1, alphas_a)
        return col_max_b, run_max, alphas_b

    col_max, run_max, alphas = lax.fori_loop(0, n_kv // 2 - 1, two_chunks,
                                             (col_max_next, run_max, alphas))
    _, alphas_last = softmax(1, col_max, run_max)
    values(n_kv - 2, 0, alphas)
    values(n_kv - 1, 1, alphas_last)

    l1, l2 = acc_ref[0, HEAD_W:HEAD_W + 1, :], acc_ref[1, HEAD_W:HEAD_W + 1, :]
    lam = (jnp.exp(jnp.sum(lq1_ref[...] * lk1_ref[...], axis=-1, keepdims=True))
           - jnp.exp(jnp.sum(lq2_ref[...] * lk2_ref[...], axis=-1, keepdims=True))
           + lambda_init)
    o_t = (acc_ref[0, :HEAD_W, :] * (1.0 / l1)
           - lam * (acc_ref[1, :HEAD_W, :] * (1.0 / l2)))
    ms = jnp.mean(o_t * o_t, axis=0, keepdims=True)
    y = (o_t * lax.rsqrt(ms + RMS_EPS)).T
    o_ref[...] = (y * g_ref[...] * (1.0 - lambda_init)).astype(BF16)


def _attention(qt4, z3, vt4, lq1, lk1, lq2, lk2, subln_g, lambda_init):
    B, n_kv, _, tk = vt4.shape
    n_q, tq = qt4.shape[1], qt4.shape[3]
    S = z3.shape[1]
    assert n_kv >= 2 and n_kv % 2 == 0, "the chunk pipeline takes an even number of key chunks"
    vec = lambda n: pl.BlockSpec((1, n), lambda b, h, i: (0, 0))
    kern = functools.partial(_attn_kernel, tk=tk, n_kv=n_kv, lambda_init=lambda_init)
    slot = lambda dt: pltpu.VMEM((2, tk, tq), dt)
    return pl.pallas_call(
        kern,
        grid=(B, N_HEADS, n_q),
        in_specs=[
            pl.BlockSpec((None, None, HEAD_W, tq), lambda b, h, i: (b, i, h, 0)),
            pl.BlockSpec((None, S, HEAD_W), lambda b, h, i: (b, 0, h)),
            pl.BlockSpec((None, n_kv, VT_ROWS, tk), lambda b, h, i: (b, 0, h, 0)),
            vec(DH), vec(DH), vec(DH), vec(DH),
            vec(HEAD_W),
        ],
        out_specs=pl.BlockSpec((None, tq, HEAD_W), lambda b, h, i: (b, i, h)),
        out_shape=jax.ShapeDtypeStruct((B, S, ATTN_W), BF16),
        scratch_shapes=[slot(F32), slot(F32), slot(BF16), slot(BF16),
                        pltpu.VMEM((2, VT_ROWS, tq), F32)],
        compiler_params=_params(("parallel", "parallel", "arbitrary")),
    )(qt4, z3, vt4, lq1, lk1, lq2, lk2, subln_g)


def _merge_kernel(x_ref, osgu_ref, oa_ref, g_ref, wga_ref, wgs_ref, wba_ref, wbs_ref, wo_ref,
                  y_ref, h_ref):
    @pl.when(pl.program_id(1) == 0)
    def _():
        x = x_ref[...]
        y_ref[...] = x
        h_ref[...] = _rms(x, g_ref[...]).astype(BF16)

    h = h_ref[...]
    ga = jax.nn.sigmoid(jnp.dot(h, wga_ref[...], preferred_element_type=F32))
    gs = jax.nn.sigmoid(jnp.dot(h, wgs_ref[...], preferred_element_type=F32))
    ba = jnp.dot(oa_ref[...], wba_ref[...], preferred_element_type=F32)
    bs = jnp.dot(osgu_ref[...], wbs_ref[...], preferred_element_type=F32)
    merged = (ga * ba + gs * bs).astype(BF16)
    y_ref[...] += jnp.dot(merged, wo_ref[...], preferred_element_type=F32)


def _merge(x2, z, oa, norm_g, w_gates, wba, wbs, wo, tm, tn):
    T = x2.shape[0]
    return pl.pallas_call(
        _merge_kernel,
        grid=(T // tm, D_MODEL // tn),
        in_specs=[
            pl.BlockSpec((tm, D_MODEL), lambda i, j: (i, 0)),
            pl.BlockSpec((tm, SGU_W), lambda i, j: (i, ATTN_W // SGU_W)),
            pl.BlockSpec((tm, ATTN_W), lambda i, j: (i, 0)),
            pl.BlockSpec((1, D_MODEL), lambda i, j: (0, 0)),
            pl.BlockSpec((D_MODEL, tn), lambda i, j: (0, j)),
            pl.BlockSpec((D_MODEL, tn), lambda i, j: (0, D_MODEL // tn + j)),
            pl.BlockSpec((ATTN_W, tn), lambda i, j: (0, j)),
            pl.BlockSpec((SGU_W, tn), lambda i, j: (0, j)),
            pl.BlockSpec((tn, D_MODEL), lambda i, j: (j, 0)),
        ],
        out_specs=pl.BlockSpec((tm, D_MODEL), lambda i, j: (i, 0)),
        out_shape=jax.ShapeDtypeStruct((T, D_MODEL), F32),
        scratch_shapes=[pltpu.VMEM((tm, D_MODEL), BF16)],
        compiler_params=_params(("parallel", "arbitrary")),
    )(x2, z, oa, norm_g, w_gates, w_gates, wba, wbs, wo)


def _mlp_kernel(x_ref, g_ref, wu_hbm, wd_hbm, fg_ref, y_ref, h_ref, wu_buf, wd_buf, sem,
                *, n_tiles, n_blocks, final_norm):
    j = pl.program_id(1)
    step = pl.program_id(0) * n_blocks + j
    tf = wu_buf.shape[2]

    def weight_copies(s):
        blk = pl.multiple_of(lax.rem(s, n_blocks) * tf, tf)
        slot = lax.rem(s, WEIGHT_SLOTS)
        return (pltpu.make_async_copy(wu_hbm.at[:, pl.ds(blk, tf)], wu_buf.at[slot], sem.at[0, slot]),
                pltpu.make_async_copy(wd_hbm.at[pl.ds(blk, tf), :], wd_buf.at[slot], sem.at[1, slot]))

    def start(s):
        for copy in weight_copies(s):
            copy.start()

    @pl.when(step == 0)
    def _():
        for s in range(WEIGHT_SLOTS - 1):
            start(s)

    ahead = step + (WEIGHT_SLOTS - 1)

    @pl.when(ahead < n_tiles * n_blocks)
    def _():
        start(ahead)

    @pl.when(j == 0)
    def _():
        x = x_ref[...]
        y_ref[...] = x
        h_ref[...] = _rms(x, g_ref[...]).astype(BF16)

    for copy in weight_copies(step):
        copy.wait()
    slot = lax.rem(step, WEIGHT_SLOTS)
    u = jnp.dot(h_ref[...], wu_buf[slot], preferred_element_type=F32)
    u = jnp.square(jnp.maximum(u, 0.0)).astype(BF16)
    y_ref[...] += jnp.dot(u, wd_buf[slot], preferred_element_type=F32)

    if final_norm:
        @pl.when(j == n_blocks - 1)
        def _():
            y_ref[...] = _rms(y_ref[...], fg_ref[...])


def _mlp(x2, norm_g, wu, wd, final_g, final_norm, tm, tf):
    T = x2.shape[0]
    n_tiles, n_blocks = T // tm, D_FF // tf
    assert n_tiles * n_blocks >= WEIGHT_SLOTS
    kern = functools.partial(_mlp_kernel, n_tiles=n_tiles, n_blocks=n_blocks,
                             final_norm=final_norm)
    return pl.pallas_call(
        kern,
        grid=(n_tiles, n_blocks),
        in_specs=[
            pl.BlockSpec((tm, D_MODEL), lambda i, j: (i, 0)),
            pl.BlockSpec((1, D_MODEL), lambda i, j: (0, 0)),
            pl.BlockSpec(memory_space=pl.ANY),
            pl.BlockSpec(memory_space=pl.ANY),
            pl.BlockSpec((1, D_MODEL), lambda i, j: (0, 0)),
        ],
        out_specs=pl.BlockSpec((tm, D_MODEL), lambda i, j: (i, 0)),
        out_shape=jax.ShapeDtypeStruct((T, D_MODEL), F32),
        scratch_shapes=[pltpu.VMEM((tm, D_MODEL), BF16),
                        pltpu.VMEM((WEIGHT_SLOTS, D_MODEL, tf), BF16),
                        pltpu.VMEM((WEIGHT_SLOTS, tf, D_MODEL), BF16),
                        pltpu.SemaphoreType.DMA((2, WEIGHT_SLOTS))],
        compiler_params=_params(("arbitrary", "arbitrary")),
    )(x2, norm_g, wu, wd, final_g)


def _rotary_lane_order(w):
    split = ROT_DIM + DH // 2 - ROT_HALF
    return jnp.concatenate([w[..., :ROT_HALF], w[..., ROT_DIM:split],
                            w[..., ROT_HALF:ROT_DIM], w[..., split:]], axis=-1)


def _rot_tables(seq_len):
    pos = jnp.arange(seq_len, dtype=F32)
    inv_freq = 1.0 / (jnp.float32(ROPE_THETA) ** (jnp.arange(0, ROT_DIM, 2, dtype=F32) / ROT_DIM))
    ang = pos[:, None] * inv_freq[None, :]
    cos, sin = jnp.cos(ang), jnp.sin(ang)
    pad = DH // 2 - ROT_HALF
    ones, zeros = jnp.ones((seq_len, pad), F32), jnp.zeros((seq_len, pad), F32)
    rc = jnp.concatenate([cos, ones, cos, ones], axis=-1)
    rs = jnp.concatenate([-sin, zeros, sin, zeros], axis=-1)
    return rc * EXP2_SCALE, rs * EXP2_SCALE, rc, rs


def _token_tile(seq_len):
    return min(512, seq_len // 2)


def _prepare(p):
    row = lambda a: a.reshape(1, -1).astype(F32)
    layers = []
    for l in range(p["w_in"].shape[0]):
        w_in = p["w_in"][l]
        w_qk = _rotary_lane_order(w_in[:, :2 * ATTN_W].reshape(D_MODEL, -1, DH))
        w_proj = jnp.concatenate([w_qk.reshape(D_MODEL, 2 * ATTN_W),
                                  w_in[:, 2 * ATTN_W:PROJ_COLS]], axis=1).astype(BF16)
        layers.append(dict(
            attn_norm_g=row(p["attn_norm_g"][l]), w_proj=w_proj,
            w_gates=w_in[:, PROJ_COLS:].astype(BF16),
            lq1=row(p["lambda_q1"][l]), lk1=row(p["lambda_k1"][l]),
            lq2=row(p["lambda_q2"][l]), lk2=row(p["lambda_k2"][l]),
            subln_g=row(p["subln_g"][l]), ln_g=row(p["sgu_ln_g"][l]), ln_b=row(p["sgu_ln_b"][l]),
            sgu_w=p["sgu_w"][l].astype(BF16), sgu_bt=p["sgu_b"][l].T.astype(F32),
            w_br_attn=p["w_br_attn"][l].astype(BF16), w_br_sgu=p["w_br_sgu"][l].astype(BF16),
            w_out=p["w_out"][l].astype(BF16), mlp_norm_g=row(p["mlp_norm_g"][l]),
            w_up=p["w_up"][l].astype(BF16), w_down=p["w_down"][l].astype(BF16)))
    return layers, row(p["final_norm_g"])


def _trunk(x, layers, final_g):
    B, S, _ = x.shape
    tm = _token_tile(S)
    rot = _rot_tables(S)
    x2 = x.reshape(B * S, D_MODEL)
    for l, w in enumerate(layers):
        qt4, vt4, z = _in_proj(x2, B, S, w["attn_norm_g"], w["w_proj"], rot, w["ln_g"], w["ln_b"],
                               w["sgu_w"], w["sgu_bt"], tm)
        oa = _attention(qt4, z.reshape(B, S, ZCOLS), vt4, w["lq1"], w["lk1"], w["lq2"], w["lk2"],
                        w["subln_g"], _lambda_init(l))
        x2 = _merge(x2, z, oa.reshape(B * S, ATTN_W), w["attn_norm_g"], w["w_gates"],
                    w["w_br_attn"], w["w_br_sgu"], w["w_out"], tm, MERGE_TN)
        x2 = _mlp(x2, w["mlp_norm_g"], w["w_up"], w["w_down"], final_g,
                  l == len(layers) - 1, tm, MLP_TF)
    return x2.reshape(B, S, D_MODEL)


def kernel(x_prompt, x_sample, attn_norm_g, w_in, lambda_q1, lambda_k1, lambda_q2, lambda_k2,
           subln_g, sgu_ln_g, sgu_ln_b, sgu_w, sgu_b, w_br_attn, w_br_sgu, w_out, mlp_norm_g,
           w_up, w_down, final_norm_g):
    p = dict(attn_norm_g=attn_norm_g, w_in=w_in, lambda_q1=lambda_q1, lambda_k1=lambda_k1,
             lambda_q2=lambda_q2, lambda_k2=lambda_k2, subln_g=subln_g, sgu_ln_g=sgu_ln_g,
             sgu_ln_b=sgu_ln_b, sgu_w=sgu_w, sgu_b=sgu_b, w_br_attn=w_br_attn, w_br_sgu=w_br_sgu,
             w_out=w_out, mlp_norm_g=mlp_norm_g, w_up=w_up, w_down=w_down,
             final_norm_g=final_norm_g)
    layers, final_g = _prepare(p)
    return _trunk(x_prompt, layers, final_g), _trunk(x_sample, layers, final_g)
```

```python
import functools
import math

import jax
import jax.numpy as jnp
from jax import lax
from jax.experimental import pallas as pl
from jax.experimental.pallas import tpu as pltpu

D_MODEL = 2048
N_HEADS = 4
DH = 128
HEAD_W = 2 * DH
ATTN_W = N_HEADS * HEAD_W
ONES_ROWS = 16
VT_ROWS = HEAD_W + ONES_ROWS
ROT_DIM = DH // 4
ROT_HALF = ROT_DIM // 2
ROPE_THETA = 500000.0
ATTN_SCALE = DH ** -0.5
CHUNK = 128
SGU_GROUPS = 8
SGU_W = SGU_GROUPS * CHUNK
D_FF = 4 * D_MODEL
PROJ_COLS = 3 * ATTN_W + 2 * SGU_W
ZCOLS = ATTN_W + SGU_W
RMS_EPS = 1e-6
LN_EPS = 1e-5
SQRT_HALF = math.sqrt(0.5)
EXP2_SCALE = ATTN_SCALE * math.log2(math.e)

VMEM_LIMIT_BYTES_V7X = 56 * 1024 * 1024

MERGE_TM = 256
MERGE_TN = 512
MLP_TF = 1024

F32 = jnp.float32
BF16 = jnp.bfloat16


def _lambda_init(layer_idx):
    return 0.8 - 0.6 * math.exp(-0.3 * layer_idx)


def _params(semantics):
    return pltpu.CompilerParams(dimension_semantics=semantics,
                                vmem_limit_bytes=VMEM_LIMIT_BYTES_V7X)


def _rms(x, g):
    ms = jnp.mean(x * x, axis=-1, keepdims=True)
    return x * lax.rsqrt(ms + RMS_EPS) * g


def _gelu(x):
    return 0.5 * x * (1.0 + lax.erf(x * SQRT_HALF))


def _in_proj_kernel(x_ref, g_ref, w_ref, rcq_ref, rsq_ref, rck_ref, rsk_ref, lng_ref, lnb_ref,
                    sw_ref, sbt_ref, qt_ref, vt_ref, z_ref, h_ref, su_ref, svn_ref):
    tm = x_ref.shape[0]
    h_ref[...] = _rms(x_ref[...], g_ref[...]).astype(BF16)

    def proj(c):
        return jnp.dot(h_ref[...], w_ref[:, c * ATTN_W:(c + 1) * ATTN_W],
                       preferred_element_type=F32)

    def rotary(acc, rc, rs, u):
        t = acc[:, u * DH:(u + 1) * DH]
        return t * rc + pltpu.roll(t, DH // 2, 1) * rs

    acc = proj(0)
    rc, rs = rcq_ref[...], rsq_ref[...]
    for u in range(ATTN_W // DH):
        qt_ref[u * DH:(u + 1) * DH, :] = rotary(acc, rc, rs, u).T.astype(BF16)

    acc = proj(1)
    rc, rs = rck_ref[...], rsk_ref[...]
    for u in range(ATTN_W // DH):
        z_ref[:, u * DH:(u + 1) * DH] = rotary(acc, rc, rs, u).astype(BF16)

    vt = proj(2).T.astype(BF16)
    for hd in range(N_HEADS):
        vt_ref[hd * VT_ROWS:hd * VT_ROWS + HEAD_W, :] = vt[hd * HEAD_W:(hd + 1) * HEAD_W, :]
        vt_ref[hd * VT_ROWS + HEAD_W:(hd + 1) * VT_ROWS, :] = jnp.ones((ONES_ROWS, tm), BF16)

    su_ref[...] = _gelu(proj(3))
    y = _gelu(proj(4))
    mu = jnp.mean(y, axis=-1, keepdims=True)
    yc = y - mu
    var = jnp.mean(yc * yc, axis=-1, keepdims=True)
    svn_ref[...] = (yc * lax.rsqrt(var + LN_EPS) * lng_ref[...] + lnb_ref[...]).astype(BF16)

    for r in range(tm // CHUNK):
        rows = slice(r * CHUNK, (r + 1) * CHUNK)
        for g in range(SGU_GROUPS):
            cols = slice(g * CHUNK, (g + 1) * CHUNK)
            mixed = jnp.dot(sw_ref[g], svn_ref[rows, cols], preferred_element_type=F32)
            mixed = mixed + sbt_ref[:, g:g + 1]
            out_cols = slice(ATTN_W + g * CHUNK, ATTN_W + (g + 1) * CHUNK)
            z_ref[rows, out_cols] = (su_ref[rows, cols] * mixed).astype(BF16)


def _in_proj(x2, batch, seq_len, norm_g, w_proj, rot, ln_g, ln_b, sgu_w_bf, sgu_bt, tm):
    T = x2.shape[0]
    n_pos_blocks = seq_len // tm
    rot_spec = pl.BlockSpec((tm, DH), lambda i: (i % n_pos_blocks, 0))
    whole = lambda a: pl.BlockSpec(a.shape, lambda i: (0,) * a.ndim)
    t_spec = lambda rows: pl.BlockSpec((None, None, rows, tm),
                                       lambda i: (i // n_pos_blocks, i % n_pos_blocks, 0, 0))
    t_shape = lambda rows: jax.ShapeDtypeStruct((batch, n_pos_blocks, rows, tm), BF16)
    return pl.pallas_call(
        _in_proj_kernel,
        grid=(T // tm,),
        in_specs=[
            pl.BlockSpec((tm, D_MODEL), lambda i: (i, 0)),
            whole(norm_g), whole(w_proj),
            rot_spec, rot_spec, rot_spec, rot_spec,
            whole(ln_g), whole(ln_b), whole(sgu_w_bf), whole(sgu_bt),
        ],
        out_specs=[t_spec(ATTN_W), t_spec(N_HEADS * VT_ROWS),
                   pl.BlockSpec((tm, ZCOLS), lambda i: (i, 0))],
        out_shape=[t_shape(ATTN_W), t_shape(N_HEADS * VT_ROWS),
                   jax.ShapeDtypeStruct((T, ZCOLS), BF16)],
        scratch_shapes=[pltpu.VMEM((tm, D_MODEL), BF16),
                        pltpu.VMEM((tm, SGU_W), F32),
                        pltpu.VMEM((tm, SGU_W), BF16)],
        compiler_params=_params(("parallel",)),
    )(x2, norm_g, w_proj, rot[0], rot[1], rot[2], rot[3], ln_g, ln_b, sgu_w_bf, sgu_bt)


def _attn_kernel(qt_ref, k_ref, vt_ref, lq1_ref, lk1_ref, lq2_ref, lk2_ref, g_ref, o_ref,
                 s0_ref, s1_ref, p0_ref, p1_ref, acc_ref, *, tk, n_kv, lambda_init):
    s_refs, p_refs = (s0_ref, s1_ref), (p0_ref, p1_ref)

    def scores(j, slot):
        rows = pl.ds(pl.multiple_of(j * tk, tk), tk)
        col_max = []
        for c in range(2):
            s = jnp.dot(k_ref[rows, c * DH:(c + 1) * DH], qt_ref[c * DH:(c + 1) * DH, :],
                        preferred_element_type=F32)
            s_refs[slot][c] = s
            col_max.append(jnp.max(s, axis=0, keepdims=True))
        return tuple(col_max)

    def softmax(slot, col_max, run_max):
        new_max, alphas = [], []
        for c in range(2):
            m_new = jnp.maximum(run_max[c], col_max[c])
            alphas.append(jnp.exp2(run_max[c] - m_new))
            p_refs[slot][c] = jnp.exp2(s_refs[slot][c] - m_new).astype(BF16)
            new_max.append(m_new)
        return tuple(new_max), tuple(alphas)

    def values(j, slot, alphas):
        vt = vt_ref[j]
        for c in range(2):
            pv = jnp.dot(vt, p_refs[slot][c], preferred_element_type=F32)
            acc_ref[c] = alphas[c] * acc_ref[c] + pv

    tq = qt_ref.shape[1]
    acc_ref[...] = jnp.zeros(acc_ref.shape, F32)
    run_max = tuple(jnp.full((1, tq), -jnp.inf, F32) for _ in range(2))

    col_max = scores(0, 0)
    col_max_next = scores(1, 1)
    run_max, alphas = softmax(0, col_max, run_max)

    def two_chunks(u, carry):
        col_max, run_max, alphas = carry
        col_max_a = scores(2 * u + 2, 0)
        run_max, alphas_a = softmax(1, col_max, run_max)
        values(2 * u, 0, alphas)
        col_max_b = scores(2 * u + 3, 1)
        run_max, alphas_b = softmax(0, col_max_a, run_max)
        values(2 * u + 1, 1, alphas_a)
        return col_max_b, run_max, alphas_b

    col_max, run_max, alphas = lax.fori_loop(0, n_kv // 2 - 1, two_chunks,
                                             (col_max_next, run_max, alphas))
    _, alphas_last = softmax(1, col_max, run_max)
    values(n_kv - 2, 0, alphas)
    values(n_kv - 1, 1, alphas_last)

    l1, l2 = acc_ref[0, HEAD_W:HEAD_W + 1, :], acc_ref[1, HEAD_W:HEAD_W + 1, :]
    lam = (jnp.exp(jnp.sum(lq1_ref[...] * lk1_ref[...], axis=-1, keepdims=True))
           - jnp.exp(jnp.sum(lq2_ref[...] * lk2_ref[...], axis=-1, keepdims=True))
           + lambda_init)
    o_t = (acc_ref[0, :HEAD_W, :] * (1.0 / l1)
           - lam * (acc_ref[1, :HEAD_W, :] * (1.0 / l2)))
    ms = jnp.mean(o_t * o_t, axis=0, keepdims=True)
    y = (o_t * lax.rsqrt(ms + RMS_EPS)).T
    o_ref[...] = (y * g_ref[...] * (1.0 - lambda_init)).astype(BF16)


def _attention(qt4, z3, vt4, lq1, lk1, lq2, lk2, subln_g, lambda_init):
    B, n_kv, _, tk = vt4.shape
    n_q, tq = qt4.shape[1], qt4.shape[3]
    S = z3.shape[1]
    assert n_kv >= 2 and n_kv % 2 == 0, "the chunk pipeline takes an even number of key chunks"
    vec = lambda n: pl.BlockSpec((1, n), lambda b, h, i: (0, 0))
    kern = functools.partial(_attn_kernel, tk=tk, n_kv=n_kv, lambda_init=lambda_init)
    slot = lambda dt: pltpu.VMEM((2, tk, tq), dt)
    return pl.pallas_call(
        kern,
        grid=(B, N_HEADS, n_q),
        in_specs=[
            pl.BlockSpec((None, None, HEAD_W, tq), lambda b, h, i: (b, i, h, 0)),
            pl.BlockSpec((None, S, HEAD_W), lambda b, h, i: (b, 0, h)),
            pl.BlockSpec((None, n_kv, VT_ROWS, tk), lambda b, h, i: (b, 0, h, 0)),
            vec(DH), vec(DH), vec(DH), vec(DH),
            vec(HEAD_W),
        ],
        out_specs=pl.BlockSpec((None, tq, HEAD_W), lambda b, h, i: (b, i, h)),
        out_shape=jax.ShapeDtypeStruct((B, S, ATTN_W), BF16),
        scratch_shapes=[slot(F32), slot(F32), slot(BF16), slot(BF16),
                        pltpu.VMEM((2, VT_ROWS, tq), F32)],
        compiler_params=_params(("parallel", "parallel", "arbitrary")),
    )(qt4, z3, vt4, lq1, lk1, lq2, lk2, subln_g)


def _merge_kernel(x_ref, osgu_ref, oa_ref, g_ref, wg_ref, wba_ref, wbs_ref, wo_ref,
                  y_ref, h_ref, m_ref, *, tn):
    x = x_ref[...]
    h_ref[...] = _rms(x, g_ref[...]).astype(BF16)
    for j in range(D_MODEL // tn):
        cols = slice(j * tn, (j + 1) * tn)
        gate_s_cols = slice(D_MODEL + j * tn, D_MODEL + (j + 1) * tn)
        ga = jax.nn.sigmoid(jnp.dot(h_ref[...], wg_ref[:, cols], preferred_element_type=F32))
        gs = jax.nn.sigmoid(jnp.dot(h_ref[...], wg_ref[:, gate_s_cols],
                                    preferred_element_type=F32))
        ba = jnp.dot(oa_ref[...], wba_ref[:, cols], preferred_element_type=F32)
        bs = jnp.dot(osgu_ref[...], wbs_ref[:, cols], preferred_element_type=F32)
        m_ref[:, cols] = (ga * ba + gs * bs).astype(BF16)
    y_ref[...] = x + jnp.dot(m_ref[...], wo_ref[...], preferred_element_type=F32)


def _merge(x2, z, oa, norm_g, w_gates, wba, wbs, wo, tm, tn):
    T = x2.shape[0]
    whole = lambda a: pl.BlockSpec(a.shape, lambda i: (0,) * a.ndim)
    return pl.pallas_call(
        functools.partial(_merge_kernel, tn=tn),
        grid=(T // tm,),
        in_specs=[
            pl.BlockSpec((tm, D_MODEL), lambda i: (i, 0)),
            pl.BlockSpec((tm, SGU_W), lambda i: (i, ATTN_W // SGU_W)),
            pl.BlockSpec((tm, ATTN_W), lambda i: (i, 0)),
            whole(norm_g), whole(w_gates), whole(wba), whole(wbs), whole(wo),
        ],
        out_specs=pl.BlockSpec((tm, D_MODEL), lambda i: (i, 0)),
        out_shape=jax.ShapeDtypeStruct((T, D_MODEL), F32),
        scratch_shapes=[pltpu.VMEM((tm, D_MODEL), BF16), pltpu.VMEM((tm, D_MODEL), BF16)],
        compiler_params=_params(("parallel",)),
    )(x2, z, oa, norm_g, w_gates, wba, wbs, wo)


def _mlp_kernel(x_ref, g_ref, wu_ref, wd_ref, fg_ref, y_ref, h_ref, *, final_norm):
    j = pl.program_id(1)

    @pl.when(j == 0)
    def _():
        x = x_ref[...]
        y_ref[...] = x
        h_ref[...] = _rms(x, g_ref[...]).astype(BF16)

    u = jnp.dot(h_ref[...], wu_ref[...], preferred_element_type=F32)
    u = jnp.square(jnp.maximum(u, 0.0)).astype(BF16)
    y_ref[...] += jnp.dot(u, wd_ref[...], preferred_element_type=F32)

    if final_norm:
        @pl.when(j == pl.num_programs(1) - 1)
        def _():
            y_ref[...] = _rms(y_ref[...], fg_ref[...])


def _mlp(x2, norm_g, wu, wd, final_g, final_norm, tm, tf):
    T = x2.shape[0]
    return pl.pallas_call(
        functools.partial(_mlp_kernel, final_norm=final_norm),
        grid=(T // tm, D_FF // tf),
        in_specs=[
            pl.BlockSpec((tm, D_MODEL), lambda i, j: (i, 0)),
            pl.BlockSpec((1, D_MODEL), lambda i, j: (0, 0)),
            pl.BlockSpec((D_MODEL, tf), lambda i, j: (0, j)),
            pl.BlockSpec((tf, D_MODEL), lambda i, j: (j, 0)),
            pl.BlockSpec((1, D_MODEL), lambda i, j: (0, 0)),
        ],
        out_specs=pl.BlockSpec((tm, D_MODEL), lambda i, j: (i, 0)),
        out_shape=jax.ShapeDtypeStruct((T, D_MODEL), F32),
        scratch_shapes=[pltpu.VMEM((tm, D_MODEL), BF16)],
        compiler_params=_params(("parallel", "arbitrary")),
    )(x2, norm_g, wu, wd, final_g)


def _rotary_lane_order(w):
    split = ROT_DIM + DH // 2 - ROT_HALF
    return jnp.concatenate([w[..., :ROT_HALF], w[..., ROT_DIM:split],
                            w[..., ROT_HALF:ROT_DIM], w[..., split:]], axis=-1)


def _rot_tables(seq_len):
    pos = jnp.arange(seq_len, dtype=F32)
    inv_freq = 1.0 / (jnp.float32(ROPE_THETA) ** (jnp.arange(0, ROT_DIM, 2, dtype=F32) / ROT_DIM))
    ang = pos[:, None] * inv_freq[None, :]
    cos, sin = jnp.cos(ang), jnp.sin(ang)
    pad = DH // 2 - ROT_HALF
    ones, zeros = jnp.ones((seq_len, pad), F32), jnp.zeros((seq_len, pad), F32)
    rc = jnp.concatenate([cos, ones, cos, ones], axis=-1)
    rs = jnp.concatenate([-sin, zeros, sin, zeros], axis=-1)
    return rc * EXP2_SCALE, rs * EXP2_SCALE, rc, rs


def _token_tile(seq_len):
    return min(512, seq_len // 2)


def _prepare(p):
    row = lambda a: a.reshape(1, -1).astype(F32)
    layers = []
    for l in range(p["w_in"].shape[0]):
        w_in = p["w_in"][l]
        w_qk = _rotary_lane_order(w_in[:, :2 * ATTN_W].reshape(D_MODEL, -1, DH))
        w_proj = jnp.concatenate([w_qk.reshape(D_MODEL, 2 * ATTN_W),
                                  w_in[:, 2 * ATTN_W:PROJ_COLS]], axis=1).astype(BF16)
        layers.append(dict(
            attn_norm_g=row(p["attn_norm_g"][l]), w_proj=w_proj,
            w_gates=w_in[:, PROJ_COLS:].astype(BF16),
            lq1=row(p["lambda_q1"][l]), lk1=row(p["lambda_k1"][l]),
            lq2=row(p["lambda_q2"][l]), lk2=row(p["lambda_k2"][l]),
            subln_g=row(p["subln_g"][l]), ln_g=row(p["sgu_ln_g"][l]), ln_b=row(p["sgu_ln_b"][l]),
            sgu_w=p["sgu_w"][l].astype(BF16), sgu_bt=p["sgu_b"][l].T.astype(F32),
            w_br_attn=p["w_br_attn"][l].astype(BF16), w_br_sgu=p["w_br_sgu"][l].astype(BF16),
            w_out=p["w_out"][l].astype(BF16), mlp_norm_g=row(p["mlp_norm_g"][l]),
            w_up=p["w_up"][l].astype(BF16), w_down=p["w_down"][l].astype(BF16)))
    return layers, row(p["final_norm_g"])


def _trunk(x, layers, final_g):
    B, S, _ = x.shape
    tm = _token_tile(S)
    rot = _rot_tables(S)
    x2 = x.reshape(B * S, D_MODEL)
    for l, w in enumerate(layers):
        qt4, vt4, z = _in_proj(x2, B, S, w["attn_norm_g"], w["w_proj"], rot, w["ln_g"], w["ln_b"],
                               w["sgu_w"], w["sgu_bt"], tm)
        oa = _attention(qt4, z.reshape(B, S, ZCOLS), vt4, w["lq1"], w["lk1"], w["lq2"], w["lk2"],
                        w["subln_g"], _lambda_init(l))
        x2 = _merge(x2, z, oa.reshape(B * S, ATTN_W), w["attn_norm_g"], w["w_gates"],
                    w["w_br_attn"], w["w_br_sgu"], w["w_out"], MERGE_TM, MERGE_TN)
        x2 = _mlp(x2, w["mlp_norm_g"], w["w_up"], w["w_down"], final_g,
                  l == len(layers) - 1, tm, MLP_TF)
    return x2.reshape(B, S, D_MODEL)


def kernel(x_prompt, x_sample, attn_norm_g, w_in, lambda_q1, lambda_k1, lambda_q2, lambda_k2,
           subln_g, sgu_ln_g, sgu_ln_b, sgu_w, sgu_b, w_br_attn, w_br_sgu, w_out, mlp_norm_g,
           w_up, w_down, final_norm_g):
    p = dict(attn_norm_g=attn_norm_g, w_in=w_in, lambda_q1=lambda_q1, lambda_k1=lambda_k1,
             lambda_q2=lambda_q2, lambda_k2=lambda_k2, subln_g=subln_g, sgu_ln_g=sgu_ln_g,
             sgu_ln_b=sgu_ln_b, sgu_w=sgu_w, sgu_b=sgu_b, w_br_attn=w_br_attn, w_br_sgu=w_br_sgu,
             w_out=w_out, mlp_norm_g=mlp_norm_g, w_up=w_up, w_down=w_down,
             final_norm_g=final_norm_g)
    layers, final_g = _prepare(p)
    return _trunk(x_prompt, layers, final_g), _trunk(x_sample, layers, final_g)
```

```python
import functools
import math

import jax
import jax.numpy as jnp
from jax import lax
from jax.experimental import pallas as pl
from jax.experimental.pallas import tpu as pltpu

D_MODEL = 2048
N_HEADS = 4
DH = 128
HEAD_W = 2 * DH
ATTN_W = N_HEADS * HEAD_W
ONES_ROWS = 16
VT_ROWS = HEAD_W + ONES_ROWS
ROT_DIM = DH // 4
ROT_HALF = ROT_DIM // 2
ROPE_THETA = 500000.0
ATTN_SCALE = DH ** -0.5
CHUNK = 128
SGU_GROUPS = 8
SGU_W = SGU_GROUPS * CHUNK
D_FF = 4 * D_MODEL
PROJ_COLS = 3 * ATTN_W + 2 * SGU_W
ZCOLS = ATTN_W + SGU_W
RMS_EPS = 1e-6
LN_EPS = 1e-5
SQRT_HALF = math.sqrt(0.5)
EXP2_SCALE = ATTN_SCALE * math.log2(math.e)

VMEM_LIMIT_BYTES_V7X = 56 * 1024 * 1024

MERGE_TM = 256
MERGE_TN = 512
MLP_TM = 256
MLP_TF = 1024

F32 = jnp.float32
BF16 = jnp.bfloat16


def _lambda_init(layer_idx):
    return 0.8 - 0.6 * math.exp(-0.3 * layer_idx)


def _params(semantics):
    return pltpu.CompilerParams(dimension_semantics=semantics,
                                vmem_limit_bytes=VMEM_LIMIT_BYTES_V7X)


def _rms(x, g):
    ms = jnp.mean(x * x, axis=-1, keepdims=True)
    return x * lax.rsqrt(ms + RMS_EPS) * g


def _gelu(x):
    return 0.5 * x * (1.0 + lax.erf(x * SQRT_HALF))


def _in_proj_kernel(x_ref, g_ref, w_ref, rcq_ref, rsq_ref, rck_ref, rsk_ref, lng_ref, lnb_ref,
                    sw_ref, sbt_ref, qt_ref, vt_ref, z_ref, h_ref, su_ref, svn_ref):
    tm = x_ref.shape[0]
    h_ref[...] = _rms(x_ref[...], g_ref[...]).astype(BF16)

    def proj(c):
        return jnp.dot(h_ref[...], w_ref[:, c * ATTN_W:(c + 1) * ATTN_W],
                       preferred_element_type=F32)

    def rotary(acc, rc, rs, u):
        t = acc[:, u * DH:(u + 1) * DH]
        return t * rc + pltpu.roll(t, DH // 2, 1) * rs

    acc = proj(0)
    rc, rs = rcq_ref[...], rsq_ref[...]
    for u in range(ATTN_W // DH):
        qt_ref[u * DH:(u + 1) * DH, :] = rotary(acc, rc, rs, u).T.astype(BF16)

    acc = proj(1)
    rc, rs = rck_ref[...], rsk_ref[...]
    for u in range(ATTN_W // DH):
        z_ref[:, u * DH:(u + 1) * DH] = rotary(acc, rc, rs, u).astype(BF16)

    vt = proj(2).T.astype(BF16)
    for hd in range(N_HEADS):
        vt_ref[hd * VT_ROWS:hd * VT_ROWS + HEAD_W, :] = vt[hd * HEAD_W:(hd + 1) * HEAD_W, :]
        vt_ref[hd * VT_ROWS + HEAD_W:(hd + 1) * VT_ROWS, :] = jnp.ones((ONES_ROWS, tm), BF16)

    su_ref[...] = _gelu(proj(3))
    y = _gelu(proj(4))
    mu = jnp.mean(y, axis=-1, keepdims=True)
    yc = y - mu
    var = jnp.mean(yc * yc, axis=-1, keepdims=True)
    svn_ref[...] = (yc * lax.rsqrt(var + LN_EPS) * lng_ref[...] + lnb_ref[...]).astype(BF16)

    for r in range(tm // CHUNK):
        rows = slice(r * CHUNK, (r + 1) * CHUNK)
        for g in range(SGU_GROUPS):
            cols = slice(g * CHUNK, (g + 1) * CHUNK)
            mixed = jnp.dot(sw_ref[g], svn_ref[rows, cols], preferred_element_type=F32)
            mixed = mixed + sbt_ref[:, g:g + 1]
            out_cols = slice(ATTN_W + g * CHUNK, ATTN_W + (g + 1) * CHUNK)
            z_ref[rows, out_cols] = (su_ref[rows, cols] * mixed).astype(BF16)


def _in_proj(x2, batch, seq_len, norm_g, w_proj, rot, ln_g, ln_b, sgu_w_bf, sgu_bt, tm):
    T = x2.shape[0]
    n_pos_blocks = seq_len // tm
    rot_spec = pl.BlockSpec((tm, DH), lambda i: (i % n_pos_blocks, 0))
    whole = lambda a: pl.BlockSpec(a.shape, lambda i: (0,) * a.ndim)
    t_spec = lambda rows: pl.BlockSpec((None, None, rows, tm),
                                       lambda i: (i // n_pos_blocks, i % n_pos_blocks, 0, 0))
    t_shape = lambda rows: jax.ShapeDtypeStruct((batch, n_pos_blocks, rows, tm), BF16)
    return pl.pallas_call(
        _in_proj_kernel,
        grid=(T // tm,),
        in_specs=[
            pl.BlockSpec((tm, D_MODEL), lambda i: (i, 0)),
            whole(norm_g), whole(w_proj),
            rot_spec, rot_spec, rot_spec, rot_spec,
            whole(ln_g), whole(ln_b), whole(sgu_w_bf), whole(sgu_bt),
        ],
        out_specs=[t_spec(ATTN_W), t_spec(N_HEADS * VT_ROWS),
                   pl.BlockSpec((tm, ZCOLS), lambda i: (i, 0))],
        out_shape=[t_shape(ATTN_W), t_shape(N_HEADS * VT_ROWS),
                   jax.ShapeDtypeStruct((T, ZCOLS), BF16)],
        scratch_shapes=[pltpu.VMEM((tm, D_MODEL), BF16),
                        pltpu.VMEM((tm, SGU_W), F32),
                        pltpu.VMEM((tm, SGU_W), BF16)],
        compiler_params=_params(("parallel",)),
    )(x2, norm_g, w_proj, rot[0], rot[1], rot[2], rot[3], ln_g, ln_b, sgu_w_bf, sgu_bt)


def _attn_kernel(qt_ref, k_ref, vt_ref, lq1_ref, lk1_ref, lq2_ref, lk2_ref, g_ref, o_ref,
                 s0_ref, s1_ref, p0_ref, p1_ref, acc_ref, *, tk, n_kv, lambda_init):
    s_refs, p_refs = (s0_ref, s1_ref), (p0_ref, p1_ref)

    def scores(j, slot):
        rows = pl.ds(pl.multiple_of(j * tk, tk), tk)
        col_max = []
        for c in range(2):
            s = jnp.dot(k_ref[rows, c * DH:(c + 1) * DH], qt_ref[c * DH:(c + 1) * DH, :],
                        preferred_element_type=F32)
            s_refs[slot][c] = s
            col_max.append(jnp.max(s, axis=0, keepdims=True))
        return tuple(col_max)

    def softmax(slot, col_max, run_max):
        new_max, alphas = [], []
        for c in range(2):
            m_new = jnp.maximum(run_max[c], col_max[c])
            alphas.append(jnp.exp2(run_max[c] - m_new))
            p_refs[slot][c] = jnp.exp2(s_refs[slot][c] - m_new).astype(BF16)
            new_max.append(m_new)
        return tuple(new_max), tuple(alphas)

    def values(j, slot, alphas):
        vt = vt_ref[j]
        for c in range(2):
            pv = jnp.dot(vt, p_refs[slot][c], preferred_element_type=F32)
            acc_ref[c] = alphas[c] * acc_ref[c] + pv

    tq = qt_ref.shape[1]
    acc_ref[...] = jnp.zeros(acc_ref.shape, F32)
    run_max = tuple(jnp.full((1, tq), -jnp.inf, F32) for _ in range(2))

    col_max = scores(0, 0)
    col_max_next = scores(1, 1)
    run_max, alphas = softmax(0, col_max, run_max)

    def two_chunks(u, carry):
        col_max, run_max, alphas = carry
        col_max_a = scores(2 * u + 2, 0)
        run_max, alphas_a = softmax(1, col_max, run_max)
        values(2 * u, 0, alphas)
        col_max_b = scores(2 * u + 3, 1)
        run_max, alphas_b = softmax(0, col_max_a, run_max)
        values(2 * u + 1, 1, alphas_a)
        return col_max_b, run_max, alphas_b

    col_max, run_max, alphas = lax.fori_loop(0, n_kv // 2 - 1, two_chunks,
                                             (col_max_next, run_max, alphas))
    _, alphas_last = softmax(1, col_max, run_max)
    values(n_kv - 2, 0, alphas)
    values(n_kv - 1, 1, alphas_last)

    l1, l2 = acc_ref[0, HEAD_W:HEAD_W + 1, :], acc_ref[1, HEAD_W:HEAD_W + 1, :]
    lam = (jnp.exp(jnp.sum(lq1_ref[...] * lk1_ref[...], axis=-1, keepdims=True))
           - jnp.exp(jnp.sum(lq2_ref[...] * lk2_ref[...], axis=-1, keepdims=True))
           + lambda_init)
    o_t = (acc_ref[0, :HEAD_W, :] * (1.0 / l1)
           - lam * (acc_ref[1, :HEAD_W, :] * (1.0 / l2)))
    ms = jnp.mean(o_t * o_t, axis=0, keepdims=True)
    y = (o_t * lax.rsqrt(ms + RMS_EPS)).T
    o_ref[...] = (y * g_ref[...] * (1.0 - lambda_init)).astype(BF16)


def _attention(qt4, z3, vt4, lq1, lk1, lq2, lk2, subln_g, lambda_init):
    B, n_kv, _, tk = vt4.shape
    n_q, tq = qt4.shape[1], qt4.shape[3]
    S = z3.shape[1]
    assert n_kv >= 2 and n_kv % 2 == 0, "the chunk pipeline takes an even number of key chunks"
    vec = lambda n: pl.BlockSpec((1, n), lambda b, h, i: (0, 0))
    kern = functools.partial(_attn_kernel, tk=tk, n_kv=n_kv, lambda_init=lambda_init)
    slot = lambda dt: pltpu.VMEM((2, tk, tq), dt)
    return pl.pallas_call(
        kern,
        grid=(B, N_HEADS, n_q),
        in_specs=[
            pl.BlockSpec((None, None, HEAD_W, tq), lambda b, h, i: (b, i, h, 0)),
            pl.BlockSpec((None, S, HEAD_W), lambda b, h, i: (b, 0, h)),
            pl.BlockSpec((None, n_kv, VT_ROWS, tk), lambda b, h, i: (b, 0, h, 0)),
            vec(DH), vec(DH), vec(DH), vec(DH),
            vec(HEAD_W),
        ],
        out_specs=pl.BlockSpec((None, tq, HEAD_W), lambda b, h, i: (b, i, h)),
        out_shape=jax.ShapeDtypeStruct((B, S, ATTN_W), BF16),
        scratch_shapes=[slot(F32), slot(F32), slot(BF16), slot(BF16),
                        pltpu.VMEM((2, VT_ROWS, tq), F32)],
        compiler_params=_params(("parallel", "parallel", "arbitrary")),
    )(qt4, z3, vt4, lq1, lk1, lq2, lk2, subln_g)


def _merge_kernel(x_ref, osgu_ref, oa_ref, g_ref, wg_ref, wba_ref, wbs_ref, wo_ref,
                  y_ref, h_ref, m_ref, *, tn):
    x = x_ref[...]
    h_ref[...] = _rms(x, g_ref[...]).astype(BF16)
    for j in range(D_MODEL // tn):
        cols = slice(j * tn, (j + 1) * tn)
        gate_s_cols = slice(D_MODEL + j * tn, D_MODEL + (j + 1) * tn)
        ga = jax.nn.sigmoid(jnp.dot(h_ref[...], wg_ref[:, cols], preferred_element_type=F32))
        gs = jax.nn.sigmoid(jnp.dot(h_ref[...], wg_ref[:, gate_s_cols],
                                    preferred_element_type=F32))
        ba = jnp.dot(oa_ref[...], wba_ref[:, cols], preferred_element_type=F32)
        bs = jnp.dot(osgu_ref[...], wbs_ref[:, cols], preferred_element_type=F32)
        m_ref[:, cols] = (ga * ba + gs * bs).astype(BF16)
    y_ref[...] = x + jnp.dot(m_ref[...], wo_ref[...], preferred_element_type=F32)


def _merge(x2, z, oa, norm_g, w_gates, wba, wbs, wo, tm, tn):
    T = x2.shape[0]
    whole = lambda a: pl.BlockSpec(a.shape, lambda i: (0,) * a.ndim)
    return pl.pallas_call(
        functools.partial(_merge_kernel, tn=tn),
        grid=(T // tm,),
        in_specs=[
            pl.BlockSpec((tm, D_MODEL), lambda i: (i, 0)),
            pl.BlockSpec((tm, SGU_W), lambda i: (i, ATTN_W // SGU_W)),
            pl.BlockSpec((tm, ATTN_W), lambda i: (i, 0)),
            whole(norm_g), whole(w_gates), whole(wba), whole(wbs), whole(wo),
        ],
        out_specs=pl.BlockSpec((tm, D_MODEL), lambda i: (i, 0)),
        out_shape=jax.ShapeDtypeStruct((T, D_MODEL), F32),
        scratch_shapes=[pltpu.VMEM((tm, D_MODEL), BF16), pltpu.VMEM((tm, D_MODEL), BF16)],
        compiler_params=_params(("parallel",)),
    )(x2, z, oa, norm_g, w_gates, wba, wbs, wo)


def _mlp_up_kernel(x_ref, g_ref, wu_ref, u_ref, h_ref, *, tf):
    h_ref[...] = _rms(x_ref[...], g_ref[...]).astype(BF16)
    for j in range(D_FF // tf):
        cols = slice(j * tf, (j + 1) * tf)
        u = jnp.dot(h_ref[...], wu_ref[:, cols], preferred_element_type=F32)
        u_ref[:, cols] = jnp.square(jnp.maximum(u, 0.0)).astype(BF16)


def _mlp_down_kernel(x_ref, u_ref, wd_ref, fg_ref, y_ref, *, final_norm):
    y = x_ref[...] + jnp.dot(u_ref[...], wd_ref[...], preferred_element_type=F32)
    if final_norm:
        y = _rms(y, fg_ref[...])
    y_ref[...] = y


def _mlp(x2, norm_g, wu, wd, final_g, final_norm, tm, tf):
    T = x2.shape[0]
    whole = lambda a: pl.BlockSpec(a.shape, lambda i: (0,) * a.ndim)
    rows = lambda width: pl.BlockSpec((tm, width), lambda i: (i, 0))
    u = pl.pallas_call(
        functools.partial(_mlp_up_kernel, tf=tf),
        grid=(T // tm,),
        in_specs=[rows(D_MODEL), whole(norm_g), whole(wu)],
        out_specs=rows(D_FF),
        out_shape=jax.ShapeDtypeStruct((T, D_FF), BF16),
        scratch_shapes=[pltpu.VMEM((tm, D_MODEL), BF16)],
        compiler_params=_params(("parallel",)),
    )(x2, norm_g, wu)
    return pl.pallas_call(
        functools.partial(_mlp_down_kernel, final_norm=final_norm),
        grid=(T // tm,),
        in_specs=[rows(D_MODEL), rows(D_FF), whole(wd), whole(final_g)],
        out_specs=rows(D_MODEL),
        out_shape=jax.ShapeDtypeStruct((T, D_MODEL), F32),
        compiler_params=_params(("parallel",)),
    )(x2, u, wd, final_g)


def _rotary_lane_order(w):
    split = ROT_DIM + DH // 2 - ROT_HALF
    return jnp.concatenate([w[..., :ROT_HALF], w[..., ROT_DIM:split],
                            w[..., ROT_HALF:ROT_DIM], w[..., split:]], axis=-1)


def _rot_tables(seq_len):
    pos = jnp.arange(seq_len, dtype=F32)
    inv_freq = 1.0 / (jnp.float32(ROPE_THETA) ** (jnp.arange(0, ROT_DIM, 2, dtype=F32) / ROT_DIM))
    ang = pos[:, None] * inv_freq[None, :]
    cos, sin = jnp.cos(ang), jnp.sin(ang)
    pad = DH // 2 - ROT_HALF
    ones, zeros = jnp.ones((seq_len, pad), F32), jnp.zeros((seq_len, pad), F32)
    rc = jnp.concatenate([cos, ones, cos, ones], axis=-1)
    rs = jnp.concatenate([-sin, zeros, sin, zeros], axis=-1)
    return rc * EXP2_SCALE, rs * EXP2_SCALE, rc, rs


def _token_tile(seq_len):
    return min(512, seq_len // 2)


def _prepare(p):
    row = lambda a: a.reshape(1, -1).astype(F32)
    layers = []
    for l in range(p["w_in"].shape[0]):
        w_in = p["w_in"][l]
        w_qk = _rotary_lane_order(w_in[:, :2 * ATTN_W].reshape(D_MODEL, -1, DH))
        w_proj = jnp.concatenate([w_qk.reshape(D_MODEL, 2 * ATTN_W),
                                  w_in[:, 2 * ATTN_W:PROJ_COLS]], axis=1).astype(BF16)
        layers.append(dict(
            attn_norm_g=row(p["attn_norm_g"][l]), w_proj=w_proj,
            w_gates=w_in[:, PROJ_COLS:].astype(BF16),
            lq1=row(p["lambda_q1"][l]), lk1=row(p["lambda_k1"][l]),
            lq2=row(p["lambda_q2"][l]), lk2=row(p["lambda_k2"][l]),
            subln_g=row(p["subln_g"][l]), ln_g=row(p["sgu_ln_g"][l]), ln_b=row(p["sgu_ln_b"][l]),
            sgu_w=p["sgu_w"][l].astype(BF16), sgu_bt=p["sgu_b"][l].T.astype(F32),
            w_br_attn=p["w_br_attn"][l].astype(BF16), w_br_sgu=p["w_br_sgu"][l].astype(BF16),
            w_out=p["w_out"][l].astype(BF16), mlp_norm_g=row(p["mlp_norm_g"][l]),
            w_up=p["w_up"][l].astype(BF16), w_down=p["w_down"][l].astype(BF16)))
    return layers, row(p["final_norm_g"])


def _trunk(x, layers, final_g):
    B, S, _ = x.shape
    tm = _token_tile(S)
    rot = _rot_tables(S)
    x2 = x.reshape(B * S, D_MODEL)
    for l, w in enumerate(layers):
        qt4, vt4, z = _in_proj(x2, B, S, w["attn_norm_g"], w["w_proj"], rot, w["ln_g"], w["ln_b"],
                               w["sgu_w"], w["sgu_bt"], tm)
        oa = _attention(qt4, z.reshape(B, S, ZCOLS), vt4, w["lq1"], w["lk1"], w["lq2"], w["lk2"],
                        w["subln_g"], _lambda_init(l))
        x2 = _merge(x2, z, oa.reshape(B * S, ATTN_W), w["attn_norm_g"], w["w_gates"],
                    w["w_br_attn"], w["w_br_sgu"], w["w_out"], MERGE_TM, MERGE_TN)
        x2 = _mlp(x2, w["mlp_norm_g"], w["w_up"], w["w_down"], final_g,
                  l == len(layers) - 1, MLP_TM, MLP_TF)
    return x2.reshape(B, S, D_MODEL)


def kernel(x_prompt, x_sample, attn_norm_g, w_in, lambda_q1, lambda_k1, lambda_q2, lambda_k2,
           subln_g, sgu_ln_g, sgu_ln_b, sgu_w, sgu_b, w_br_attn, w_br_sgu, w_out, mlp_norm_g,
           w_up, w_down, final_norm_g):
    p = dict(attn_norm_g=attn_norm_g, w_in=w_in, lambda_q1=lambda_q1, lambda_k1=lambda_k1,
             lambda_q2=lambda_q2, lambda_k2=lambda_k2, subln_g=subln_g, sgu_ln_g=sgu_ln_g,
             sgu_ln_b=sgu_ln_b, sgu_w=sgu_w, sgu_b=sgu_b, w_br_attn=w_br_attn, w_br_sgu=w_br_sgu,
             w_out=w_out, mlp_norm_g=mlp_norm_g, w_up=w_up, w_down=w_down,
             final_norm_g=final_norm_g)
    layers, final_g = _prepare(p)
    return _trunk(x_prompt, layers, final_g), _trunk(x_sample, layers, final_g)
```

```python
import functools
import math

import jax
import jax.numpy as jnp
from jax import lax
from jax.experimental import pallas as pl
from jax.experimental.pallas import tpu as pltpu

D_MODEL = 2048
N_HEADS = 4
DH = 128
HEAD_W = 2 * DH
ATTN_W = N_HEADS * HEAD_W
ONES_ROWS = 16
VT_ROWS = HEAD_W + ONES_ROWS
ROT_DIM = DH // 4
ROT_HALF = ROT_DIM // 2
ROPE_THETA = 500000.0
ATTN_SCALE = DH ** -0.5
CHUNK = 128
SGU_GROUPS = 8
SGU_W = SGU_GROUPS * CHUNK
D_FF = 4 * D_MODEL
PROJ_COLS = 3 * ATTN_W + 2 * SGU_W
RMS_EPS = 1e-6
LN_EPS = 1e-5
SQRT_HALF = math.sqrt(0.5)
EXP2_SCALE = ATTN_SCALE * math.log2(math.e)

VMEM_LIMIT_BYTES_V7X = 56 * 1024 * 1024

MERGE_TM = 256
MERGE_TN = 512
MLP_TM = 256
MLP_TF = 1024

F32 = jnp.float32
BF16 = jnp.bfloat16


def _lambda_init(layer_idx):
    return 0.8 - 0.6 * math.exp(-0.3 * layer_idx)


def _params(semantics):
    return pltpu.CompilerParams(dimension_semantics=semantics,
                                vmem_limit_bytes=VMEM_LIMIT_BYTES_V7X)


def _rms(x, g):
    ms = jnp.mean(x * x, axis=-1, keepdims=True)
    return x * lax.rsqrt(ms + RMS_EPS) * g


def _gelu(x):
    return 0.5 * x * (1.0 + lax.erf(x * SQRT_HALF))


def _in_proj_kernel(x_ref, g_ref, w_ref, rcq_ref, rsq_ref, rck_ref, rsk_ref, lng_ref, lnb_ref,
                    sw_ref, sbt_ref, qt_ref, vt_ref, kh_ref, osgu_ref, h_ref, su_ref, svn_ref):
    tm = x_ref.shape[0]
    h_ref[...] = _rms(x_ref[...], g_ref[...]).astype(BF16)

    def proj(c):
        return jnp.dot(h_ref[...], w_ref[:, c * ATTN_W:(c + 1) * ATTN_W],
                       preferred_element_type=F32)

    def rotary(acc, rc, rs, u):
        t = acc[:, u * DH:(u + 1) * DH]
        return t * rc + pltpu.roll(t, DH // 2, 1) * rs

    acc = proj(0)
    rc, rs = rcq_ref[...], rsq_ref[...]
    for u in range(ATTN_W // DH):
        qt_ref[u * DH:(u + 1) * DH, :] = rotary(acc, rc, rs, u).T.astype(BF16)

    acc = proj(1)
    rc, rs = rck_ref[...], rsk_ref[...]
    for u in range(ATTN_W // DH):
        kh_ref[u // 2, :, (u % 2) * DH:(u % 2 + 1) * DH] = rotary(acc, rc, rs, u).astype(BF16)

    vt = proj(2).T.astype(BF16)
    for hd in range(N_HEADS):
        vt_ref[hd * VT_ROWS:hd * VT_ROWS + HEAD_W, :] = vt[hd * HEAD_W:(hd + 1) * HEAD_W, :]
        vt_ref[hd * VT_ROWS + HEAD_W:(hd + 1) * VT_ROWS, :] = jnp.ones((ONES_ROWS, tm), BF16)

    su_ref[...] = _gelu(proj(3))
    y = _gelu(proj(4))
    mu = jnp.mean(y, axis=-1, keepdims=True)
    yc = y - mu
    var = jnp.mean(yc * yc, axis=-1, keepdims=True)
    svn_ref[...] = (yc * lax.rsqrt(var + LN_EPS) * lng_ref[...] + lnb_ref[...]).astype(BF16)

    for r in range(tm // CHUNK):
        rows = slice(r * CHUNK, (r + 1) * CHUNK)
        for g in range(SGU_GROUPS):
            cols = slice(g * CHUNK, (g + 1) * CHUNK)
            mixed = jnp.dot(sw_ref[g], svn_ref[rows, cols], preferred_element_type=F32)
            mixed = mixed + sbt_ref[:, g:g + 1]
            osgu_ref[rows, cols] = (su_ref[rows, cols] * mixed).astype(BF16)


def _in_proj(x2, batch, seq_len, norm_g, w_proj, rot, ln_g, ln_b, sgu_w_bf, sgu_bt, tm):
    T = x2.shape[0]
    n_pos_blocks = seq_len // tm
    rot_spec = pl.BlockSpec((tm, DH), lambda i: (i % n_pos_blocks, 0))
    whole = lambda a: pl.BlockSpec(a.shape, lambda i: (0,) * a.ndim)
    t_spec = lambda rows: pl.BlockSpec((None, None, rows, tm),
                                       lambda i: (i // n_pos_blocks, i % n_pos_blocks, 0, 0))
    t_shape = lambda rows: jax.ShapeDtypeStruct((batch, n_pos_blocks, rows, tm), BF16)
    return pl.pallas_call(
        _in_proj_kernel,
        grid=(T // tm,),
        in_specs=[
            pl.BlockSpec((tm, D_MODEL), lambda i: (i, 0)),
            whole(norm_g), whole(w_proj),
            rot_spec, rot_spec, rot_spec, rot_spec,
            whole(ln_g), whole(ln_b), whole(sgu_w_bf), whole(sgu_bt),
        ],
        out_specs=[t_spec(ATTN_W), t_spec(N_HEADS * VT_ROWS),
                   pl.BlockSpec((None, N_HEADS, tm, HEAD_W),
                                lambda i: (i // n_pos_blocks, 0, i % n_pos_blocks, 0)),
                   pl.BlockSpec((tm, SGU_W), lambda i: (i, 0))],
        out_shape=[t_shape(ATTN_W), t_shape(N_HEADS * VT_ROWS),
                   jax.ShapeDtypeStruct((batch, N_HEADS, seq_len, HEAD_W), BF16),
                   jax.ShapeDtypeStruct((T, SGU_W), BF16)],
        scratch_shapes=[pltpu.VMEM((tm, D_MODEL), BF16),
                        pltpu.VMEM((tm, SGU_W), F32),
                        pltpu.VMEM((tm, SGU_W), BF16)],
        compiler_params=_params(("parallel",)),
    )(x2, norm_g, w_proj, rot[0], rot[1], rot[2], rot[3], ln_g, ln_b, sgu_w_bf, sgu_bt)


def _attn_kernel(qt_ref, k_ref, vt_ref, lq1_ref, lk1_ref, lq2_ref, lk2_ref, g_ref, o_ref,
                 s0_ref, s1_ref, p0_ref, p1_ref, acc_ref, *, tk, n_kv, lambda_init):
    s_refs, p_refs = (s0_ref, s1_ref), (p0_ref, p1_ref)

    def scores(j, slot):
        rows = pl.ds(pl.multiple_of(j * tk, tk), tk)
        col_max = []
        for c in range(2):
            s = jnp.dot(k_ref[rows, c * DH:(c + 1) * DH], qt_ref[c * DH:(c + 1) * DH, :],
                        preferred_element_type=F32)
            s_refs[slot][c] = s
            col_max.append(jnp.max(s, axis=0, keepdims=True))
        return tuple(col_max)

    def softmax(slot, col_max, run_max):
        new_max, alphas = [], []
        for c in range(2):
            m_new = jnp.maximum(run_max[c], col_max[c])
            alphas.append(jnp.exp2(run_max[c] - m_new))
            p_refs[slot][c] = jnp.exp2(s_refs[slot][c] - m_new).astype(BF16)
            new_max.append(m_new)
        return tuple(new_max), tuple(alphas)

    def values(j, slot, alphas):
        vt = vt_ref[j]
        for c in range(2):
            pv = jnp.dot(vt, p_refs[slot][c], preferred_element_type=F32)
            acc_ref[c] = alphas[c] * acc_ref[c] + pv

    tq = qt_ref.shape[1]
    acc_ref[...] = jnp.zeros(acc_ref.shape, F32)
    run_max = tuple(jnp.full((1, tq), -jnp.inf, F32) for _ in range(2))

    col_max = scores(0, 0)
    col_max_next = scores(1, 1)
    run_max, alphas = softmax(0, col_max, run_max)

    def two_chunks(u, carry):
        col_max, run_max, alphas = carry
        col_max_a = scores(2 * u + 2, 0)
        run_max, alphas_a = softmax(1, col_max, run_max)
        values(2 * u, 0, alphas)
        col_max_b = scores(2 * u + 3, 1)
        run_max, alphas_b = softmax(0, col_max_a, run_max)
        values(2 * u + 1, 1, alphas_a)
        return col_max_b, run_max, alphas_b

    col_max, run_max, alphas = lax.fori_loop(0, n_kv // 2 - 1, two_chunks,
                                             (col_max_next, run_max, alphas))
    _, alphas_last = softmax(1, col_max, run_max)
    values(n_kv - 2, 0, alphas)
    values(n_kv - 1, 1, alphas_last)

    l1, l2 = acc_ref[0, HEAD_W:HEAD_W + 1, :], acc_ref[1, HEAD_W:HEAD_W + 1, :]
    lam = (jnp.exp(jnp.sum(lq1_ref[...] * lk1_ref[...], axis=-1, keepdims=True))
           - jnp.exp(jnp.sum(lq2_ref[...] * lk2_ref[...], axis=-1, keepdims=True))
           + lambda_init)
    o_t = (acc_ref[0, :HEAD_W, :] * (1.0 / l1)
           - lam * (acc_ref[1, :HEAD_W, :] * (1.0 / l2)))
    ms = jnp.mean(o_t * o_t, axis=0, keepdims=True)
    y = (o_t * lax.rsqrt(ms + RMS_EPS)).T
    o_ref[...] = (y * g_ref[...] * (1.0 - lambda_init)).astype(BF16)


def _attention(qt4, kh, vt4, lq1, lk1, lq2, lk2, subln_g, lambda_init):
    B, n_kv, _, tk = vt4.shape
    n_q, tq = qt4.shape[1], qt4.shape[3]
    S = kh.shape[2]
    assert n_kv >= 2 and n_kv % 2 == 0, "the chunk pipeline takes an even number of key chunks"
    vec = lambda n: pl.BlockSpec((1, n), lambda b, h, i: (0, 0))
    kern = functools.partial(_attn_kernel, tk=tk, n_kv=n_kv, lambda_init=lambda_init)
    slot = lambda dt: pltpu.VMEM((2, tk, tq), dt)
    return pl.pallas_call(
        kern,
        grid=(B, N_HEADS, n_q),
        in_specs=[
            pl.BlockSpec((None, None, HEAD_W, tq), lambda b, h, i: (b, i, h, 0)),
            pl.BlockSpec((None, None, S, HEAD_W), lambda b, h, i: (b, h, 0, 0)),
            pl.BlockSpec((None, n_kv, VT_ROWS, tk), lambda b, h, i: (b, 0, h, 0)),
            vec(DH), vec(DH), vec(DH), vec(DH),
            vec(HEAD_W),
        ],
        out_specs=pl.BlockSpec((None, tq, HEAD_W), lambda b, h, i: (b, i, h)),
        out_shape=jax.ShapeDtypeStruct((B, S, ATTN_W), BF16),
        scratch_shapes=[slot(F32), slot(F32), slot(BF16), slot(BF16),
                        pltpu.VMEM((2, VT_ROWS, tq), F32)],
        compiler_params=_params(("parallel", "parallel", "arbitrary")),
    )(qt4, kh, vt4, lq1, lk1, lq2, lk2, subln_g)


def _merge_kernel(x_ref, osgu_ref, oa_ref, g_ref, wg_ref, wba_ref, wbs_ref, wo_ref,
                  y_ref, h_ref, m_ref, *, tn):
    x = x_ref[...]
    h_ref[...] = _rms(x, g_ref[...]).astype(BF16)
    for j in range(D_MODEL // tn):
        cols = slice(j * tn, (j + 1) * tn)
        gate_s_cols = slice(D_MODEL + j * tn, D_MODEL + (j + 1) * tn)
        ga = jax.nn.sigmoid(jnp.dot(h_ref[...], wg_ref[:, cols], preferred_element_type=F32))
        gs = jax.nn.sigmoid(jnp.dot(h_ref[...], wg_ref[:, gate_s_cols],
                                    preferred_element_type=F32))
        ba = jnp.dot(oa_ref[...], wba_ref[:, cols], preferred_element_type=F32)
        bs = jnp.dot(osgu_ref[...], wbs_ref[:, cols], preferred_element_type=F32)
        m_ref[:, cols] = (ga * ba + gs * bs).astype(BF16)
    y_ref[...] = x + jnp.dot(m_ref[...], wo_ref[...], preferred_element_type=F32)


def _merge(x2, osgu, oa, norm_g, w_gates, wba, wbs, wo, tm, tn):
    T = x2.shape[0]
    whole = lambda a: pl.BlockSpec(a.shape, lambda i: (0,) * a.ndim)
    return pl.pallas_call(
        functools.partial(_merge_kernel, tn=tn),
        grid=(T // tm,),
        in_specs=[
            pl.BlockSpec((tm, D_MODEL), lambda i: (i, 0)),
            pl.BlockSpec((tm, SGU_W), lambda i: (i, 0)),
            pl.BlockSpec((tm, ATTN_W), lambda i: (i, 0)),
            whole(norm_g), whole(w_gates), whole(wba), whole(wbs), whole(wo),
        ],
        out_specs=pl.BlockSpec((tm, D_MODEL), lambda i: (i, 0)),
        out_shape=jax.ShapeDtypeStruct((T, D_MODEL), F32),
        scratch_shapes=[pltpu.VMEM((tm, D_MODEL), BF16), pltpu.VMEM((tm, D_MODEL), BF16)],
        compiler_params=_params(("parallel",)),
    )(x2, osgu, oa, norm_g, w_gates, wba, wbs, wo)


def _mlp_up_kernel(x_ref, g_ref, wu_ref, u_ref, h_ref, *, tf):
    h_ref[...] = _rms(x_ref[...], g_ref[...]).astype(BF16)
    for j in range(D_FF // tf):
        cols = slice(j * tf, (j + 1) * tf)
        u = jnp.dot(h_ref[...], wu_ref[:, cols], preferred_element_type=F32)
        u_ref[:, cols] = jnp.square(jnp.maximum(u, 0.0)).astype(BF16)


def _mlp_down_kernel(x_ref, u_ref, wd_ref, fg_ref, y_ref, *, final_norm):
    y = x_ref[...] + jnp.dot(u_ref[...], wd_ref[...], preferred_element_type=F32)
    if final_norm:
        y = _rms(y, fg_ref[...])
    y_ref[...] = y


def _mlp(x2, norm_g, wu, wd, final_g, final_norm, tm, tf):
    T = x2.shape[0]
    whole = lambda a: pl.BlockSpec(a.shape, lambda i: (0,) * a.ndim)
    rows = lambda width: pl.BlockSpec((tm, width), lambda i: (i, 0))
    u = pl.pallas_call(
        functools.partial(_mlp_up_kernel, tf=tf),
        grid=(T // tm,),
        in_specs=[rows(D_MODEL), whole(norm_g), whole(wu)],
        out_specs=rows(D_FF),
        out_shape=jax.ShapeDtypeStruct((T, D_FF), BF16),
        scratch_shapes=[pltpu.VMEM((tm, D_MODEL), BF16)],
        compiler_params=_params(("parallel",)),
    )(x2, norm_g, wu)
    return pl.pallas_call(
        functools.partial(_mlp_down_kernel, final_norm=final_norm),
        grid=(T // tm,),
        in_specs=[rows(D_MODEL), rows(D_FF), whole(wd), whole(final_g)],
        out_specs=rows(D_MODEL),
        out_shape=jax.ShapeDtypeStruct((T, D_MODEL), F32),
        compiler_params=_params(("parallel",)),
    )(x2, u, wd, final_g)


def _rotary_lane_order(w):
    split = ROT_DIM + DH // 2 - ROT_HALF
    return jnp.concatenate([w[..., :ROT_HALF], w[..., ROT_DIM:split],
                            w[..., ROT_HALF:ROT_DIM], w[..., split:]], axis=-1)


def _rot_tables(seq_len):
    pos = jnp.arange(seq_len, dtype=F32)
    inv_freq = 1.0 / (jnp.float32(ROPE_THETA) ** (jnp.arange(0, ROT_DIM, 2, dtype=F32) / ROT_DIM))
    ang = pos[:, None] * inv_freq[None, :]
    cos, sin = jnp.cos(ang), jnp.sin(ang)
    pad = DH // 2 - ROT_HALF
    ones, zeros = jnp.ones((seq_len, pad), F32), jnp.zeros((seq_len, pad), F32)
    rc = jnp.concatenate([cos, ones, cos, ones], axis=-1)
    rs = jnp.concatenate([-sin, zeros, sin, zeros], axis=-1)
    return rc * EXP2_SCALE, rs * EXP2_SCALE, rc, rs


def _token_tile(seq_len):
    return min(512, seq_len // 2)


def _prepare(p):
    row = lambda a: a.reshape(1, -1).astype(F32)
    layers = []
    for l in range(p["w_in"].shape[0]):
        w_in = p["w_in"][l]
        w_qk = _rotary_lane_order(w_in[:, :2 * ATTN_W].reshape(D_MODEL, -1, DH))
        w_proj = jnp.concatenate([w_qk.reshape(D_MODEL, 2 * ATTN_W),
                                  w_in[:, 2 * ATTN_W:PROJ_COLS]], axis=1).astype(BF16)
        layers.append(dict(
            attn_norm_g=row(p["attn_norm_g"][l]), w_proj=w_proj,
            w_gates=w_in[:, PROJ_COLS:].astype(BF16),
            lq1=row(p["lambda_q1"][l]), lk1=row(p["lambda_k1"][l]),
            lq2=row(p["lambda_q2"][l]), lk2=row(p["lambda_k2"][l]),
            subln_g=row(p["subln_g"][l]), ln_g=row(p["sgu_ln_g"][l]), ln_b=row(p["sgu_ln_b"][l]),
            sgu_w=p["sgu_w"][l].astype(BF16), sgu_bt=p["sgu_b"][l].T.astype(F32),
            w_br_attn=p["w_br_attn"][l].astype(BF16), w_br_sgu=p["w_br_sgu"][l].astype(BF16),
            w_out=p["w_out"][l].astype(BF16), mlp_norm_g=row(p["mlp_norm_g"][l]),
            w_up=p["w_up"][l].astype(BF16), w_down=p["w_down"][l].astype(BF16)))
    return layers, row(p["final_norm_g"])


def _trunk(x, layers, final_g):
    B, S, _ = x.shape
    tm = _token_tile(S)
    rot = _rot_tables(S)
    x2 = x.reshape(B * S, D_MODEL)
    for l, w in enumerate(layers):
        qt4, vt4, kh, osgu = _in_proj(x2, B, S, w["attn_norm_g"], w["w_proj"], rot, w["ln_g"],
                                      w["ln_b"], w["sgu_w"], w["sgu_bt"], tm)
        oa = _attention(qt4, kh, vt4, w["lq1"], w["lk1"], w["lq2"], w["lk2"],
                        w["subln_g"], _lambda_init(l))
        x2 = _merge(x2, osgu, oa.reshape(B * S, ATTN_W), w["attn_norm_g"], w["w_gates"],
                    w["w_br_attn"], w["w_br_sgu"], w["w_out"], MERGE_TM, MERGE_TN)
        x2 = _mlp(x2, w["mlp_norm_g"], w["w_up"], w["w_down"], final_g,
                  l == len(layers) - 1, MLP_TM, MLP_TF)
    return x2.reshape(B, S, D_MODEL)


def kernel(x_prompt, x_sample, attn_norm_g, w_in, lambda_q1, lambda_k1, lambda_q2, lambda_k2,
           subln_g, sgu_ln_g, sgu_ln_b, sgu_w, sgu_b, w_br_attn, w_br_sgu, w_out, mlp_norm_g,
           w_up, w_down, final_norm_g):
    p = dict(attn_norm_g=attn_norm_g, w_in=w_in, lambda_q1=lambda_q1, lambda_k1=lambda_k1,
             lambda_q2=lambda_q2, lambda_k2=lambda_k2, subln_g=subln_g, sgu_ln_g=sgu_ln_g,
             sgu_ln_b=sgu_ln_b, sgu_w=sgu_w, sgu_b=sgu_b, w_br_attn=w_br_attn, w_br_sgu=w_br_sgu,
             w_out=w_out, mlp_norm_g=mlp_norm_g, w_up=w_up, w_down=w_down,
             final_norm_g=final_norm_g)
    layers, final_g = _prepare(p)
    return _trunk(x_prompt, layers, final_g), _trunk(x_sample, layers, final_g)
```

```python
import functools
import math

import jax
import jax.numpy as jnp
from jax import lax
from jax.experimental import pallas as pl
from jax.experimental.pallas import tpu as pltpu

D_MODEL = 2048
N_HEADS = 4
DH = 128
HEAD_W = 2 * DH
ATTN_W = N_HEADS * HEAD_W
ONES_ROWS = 16
VT_ROWS = HEAD_W + ONES_ROWS
ROT_DIM = DH // 4
ROT_HALF = ROT_DIM // 2
ROPE_THETA = 500000.0
ATTN_SCALE = DH ** -0.5
CHUNK = 128
SGU_GROUPS = 8
SGU_W = SGU_GROUPS * CHUNK
D_FF = 4 * D_MODEL
PROJ_COLS = 3 * ATTN_W + 2 * SGU_W
ZCOLS = ATTN_W + SGU_W
RMS_EPS = 1e-6
LN_EPS = 1e-5
SQRT_HALF = math.sqrt(0.5)
EXP2_SCALE = ATTN_SCALE * math.log2(math.e)

VMEM_LIMIT_BYTES_V7X = 56 * 1024 * 1024

MERGE_TM = 256
MERGE_TN = 512
MLP_TM = 256
MLP_TF = 1024

F32 = jnp.float32
BF16 = jnp.bfloat16


def _lambda_init(layer_idx):
    return 0.8 - 0.6 * math.exp(-0.3 * layer_idx)


def _params(semantics):
    return pltpu.CompilerParams(dimension_semantics=semantics,
                                vmem_limit_bytes=VMEM_LIMIT_BYTES_V7X)


def _rms(x, g):
    ms = jnp.mean(x * x, axis=-1, keepdims=True)
    return x * lax.rsqrt(ms + RMS_EPS) * g


def _gelu(x):
    return 0.5 * x * (1.0 + lax.erf(x * SQRT_HALF))


def _in_proj_kernel(x_ref, g_ref, w_ref, rcq_ref, rsq_ref, rck_ref, rsk_ref, lng_ref, lnb_ref,
                    sw_ref, sbt_ref, qt_ref, vt_ref, z_ref, h_ref, su_ref, svn_ref):
    tm = x_ref.shape[0]
    h_ref[...] = _rms(x_ref[...], g_ref[...]).astype(BF16)

    def proj(c):
        return jnp.dot(h_ref[...], w_ref[:, c * ATTN_W:(c + 1) * ATTN_W],
                       preferred_element_type=F32)

    def rotary(acc, rc, rs, u):
        t = acc[:, u * DH:(u + 1) * DH]
        return t * rc + pltpu.roll(t, DH // 2, 1) * rs

    acc = proj(0)
    rc, rs = rcq_ref[...], rsq_ref[...]
    for u in range(ATTN_W // DH):
        qt_ref[u * DH:(u + 1) * DH, :] = rotary(acc, rc, rs, u).T.astype(BF16)

    acc = proj(1)
    rc, rs = rck_ref[...], rsk_ref[...]
    for u in range(ATTN_W // DH):
        z_ref[:, u * DH:(u + 1) * DH] = rotary(acc, rc, rs, u).astype(BF16)

    vt = proj(2).T.astype(BF16)
    for hd in range(N_HEADS):
        vt_ref[hd * VT_ROWS:hd * VT_ROWS + HEAD_W, :] = vt[hd * HEAD_W:(hd + 1) * HEAD_W, :]
        vt_ref[hd * VT_ROWS + HEAD_W:(hd + 1) * VT_ROWS, :] = jnp.ones((ONES_ROWS, tm), BF16)

    su_ref[...] = _gelu(proj(3))
    y = _gelu(proj(4))
    mu = jnp.mean(y, axis=-1, keepdims=True)
    yc = y - mu
    var = jnp.mean(yc * yc, axis=-1, keepdims=True)
    svn_ref[...] = (yc * lax.rsqrt(var + LN_EPS) * lng_ref[...] + lnb_ref[...]).astype(BF16)

    for r in range(tm // CHUNK):
        rows = slice(r * CHUNK, (r + 1) * CHUNK)
        for g in range(SGU_GROUPS):
            cols = slice(g * CHUNK, (g + 1) * CHUNK)
            mixed = jnp.dot(sw_ref[g], svn_ref[rows, cols], preferred_element_type=F32)
            mixed = mixed + sbt_ref[:, g:g + 1]
            out_cols = slice(ATTN_W + g * CHUNK, ATTN_W + (g + 1) * CHUNK)
            z_ref[rows, out_cols] = (su_ref[rows, cols] * mixed).astype(BF16)


def _in_proj(x2, batch, seq_len, norm_g, w_proj, rot, ln_g, ln_b, sgu_w_bf, sgu_bt, tm):
    T = x2.shape[0]
    n_pos_blocks = seq_len // tm
    rot_spec = pl.BlockSpec((tm, DH), lambda i: (i % n_pos_blocks, 0))
    whole = lambda a: pl.BlockSpec(a.shape, lambda i: (0,) * a.ndim)
    t_spec = lambda rows: pl.BlockSpec((None, None, rows, tm),
                                       lambda i: (i // n_pos_blocks, i % n_pos_blocks, 0, 0))
    t_shape = lambda rows: jax.ShapeDtypeStruct((batch, n_pos_blocks, rows, tm), BF16)
    return pl.pallas_call(
        _in_proj_kernel,
        grid=(T // tm,),
        in_specs=[
            pl.BlockSpec((tm, D_MODEL), lambda i: (i, 0)),
            whole(norm_g), whole(w_proj),
            rot_spec, rot_spec, rot_spec, rot_spec,
            whole(ln_g), whole(ln_b), whole(sgu_w_bf), whole(sgu_bt),
        ],
        out_specs=[t_spec(ATTN_W), t_spec(N_HEADS * VT_ROWS),
                   pl.BlockSpec((tm, ZCOLS), lambda i: (i, 0))],
        out_shape=[t_shape(ATTN_W), t_shape(N_HEADS * VT_ROWS),
                   jax.ShapeDtypeStruct((T, ZCOLS), BF16)],
        scratch_shapes=[pltpu.VMEM((tm, D_MODEL), BF16),
                        pltpu.VMEM((tm, SGU_W), F32),
                        pltpu.VMEM((tm, SGU_W), BF16)],
        compiler_params=_params(("parallel",)),
    )(x2, norm_g, w_proj, rot[0], rot[1], rot[2], rot[3], ln_g, ln_b, sgu_w_bf, sgu_bt)


def _attn_kernel(qt_ref, k_ref, vt_ref, lq1_ref, lk1_ref, lq2_ref, lk2_ref, g_ref, o_ref,
                 s0_ref, s1_ref, p0_ref, p1_ref, acc_ref, *, tk, n_kv, lambda_init):
    s_refs, p_refs = (s0_ref, s1_ref), (p0_ref, p1_ref)

    def scores(j, slot):
        rows = pl.ds(pl.multiple_of(j * tk, tk), tk)
        col_max = []
        for c in range(2):
            s = jnp.dot(k_ref[rows, c * DH:(c + 1) * DH], qt_ref[c * DH:(c + 1) * DH, :],
                        preferred_element_type=F32)
            s_refs[slot][c] = s
            col_max.append(jnp.max(s, axis=0, keepdims=True))
        return tuple(col_max)

    def softmax(slot, col_max, run_max):
        new_max, alphas = [], []
        for c in range(2):
            m_new = jnp.maximum(run_max[c], col_max[c])
            alphas.append(jnp.exp2(run_max[c] - m_new))
            p_refs[slot][c] = jnp.exp2(s_refs[slot][c] - m_new).astype(BF16)
            new_max.append(m_new)
        return tuple(new_max), tuple(alphas)

    def values(j, slot, alphas):
        vt = vt_ref[j]
        for c in range(2):
            pv = jnp.dot(vt, p_refs[slot][c], preferred_element_type=F32)
            acc_ref[c] = alphas[c] * acc_ref[c] + pv

    tq = qt_ref.shape[1]
    acc_ref[...] = jnp.zeros(acc_ref.shape, F32)
    run_max = tuple(jnp.full((1, tq), -jnp.inf, F32) for _ in range(2))

    col_max = scores(0, 0)
    col_max_next = scores(1, 1)
    run_max, alphas = softmax(0, col_max, run_max)

    def two_chunks(u, carry):
        col_max, run_max, alphas = carry
        col_max_a = scores(2 * u + 2, 0)
        run_max, alphas_a = softmax(1, col_max, run_max)
        values(2 * u, 0, alphas)
        col_max_b = scores(2 * u + 3, 1)
        run_max, alphas_b = softmax(0, col_max_a, run_max)
        values(2 * u + 1, 1, alphas_a)
        return col_max_b, run_max, alphas_b

    col_max, run_max, alphas = lax.fori_loop(0, n_kv // 2 - 1, two_chunks,
                                             (col_max_next, run_max, alphas))
    _, alphas_last = softmax(1, col_max, run_max)
    values(n_kv - 2, 0, alphas)
    values(n_kv - 1, 1, alphas_last)

    l1, l2 = acc_ref[0, HEAD_W:HEAD_W + 1, :], acc_ref[1, HEAD_W:HEAD_W + 1, :]
    lam = (jnp.exp(jnp.sum(lq1_ref[...] * lk1_ref[...], axis=-1, keepdims=True))
           - jnp.exp(jnp.sum(lq2_ref[...] * lk2_ref[...], axis=-1, keepdims=True))
           + lambda_init)
    o_t = (acc_ref[0, :HEAD_W, :] * (1.0 / l1)
           - lam * (acc_ref[1, :HEAD_W, :] * (1.0 / l2)))
    ms = jnp.mean(o_t * o_t, axis=0, keepdims=True)
    y = (o_t * lax.rsqrt(ms + RMS_EPS)).T
    o_ref[...] = (y * g_ref[...] * (1.0 - lambda_init)).astype(BF16)


def _attention(qt4, z3, vt4, lq1, lk1, lq2, lk2, subln_g, lambda_init):
    B, n_kv, _, tk = vt4.shape
    n_q, tq = qt4.shape[1], qt4.shape[3]
    S = z3.shape[1]
    assert n_kv >= 2 and n_kv % 2 == 0, "the chunk pipeline takes an even number of key chunks"
    vec = lambda n: pl.BlockSpec((1, n), lambda b, h, i: (0, 0))
    kern = functools.partial(_attn_kernel, tk=tk, n_kv=n_kv, lambda_init=lambda_init)
    slot = lambda dt: pltpu.VMEM((2, tk, tq), dt)
    return pl.pallas_call(
        kern,
        grid=(B, N_HEADS, n_q),
        in_specs=[
            pl.BlockSpec((None, None, HEAD_W, tq), lambda b, h, i: (b, i, h, 0)),
            pl.BlockSpec((None, S, HEAD_W), lambda b, h, i: (b, 0, h)),
            pl.BlockSpec((None, n_kv, VT_ROWS, tk), lambda b, h, i: (b, 0, h, 0)),
            vec(DH), vec(DH), vec(DH), vec(DH),
            vec(HEAD_W),
        ],
        out_specs=pl.BlockSpec((None, tq, HEAD_W), lambda b, h, i: (b, i, h)),
        out_shape=jax.ShapeDtypeStruct((B, S, ATTN_W), BF16),
        scratch_shapes=[slot(F32), slot(F32), slot(BF16), slot(BF16),
                        pltpu.VMEM((2, VT_ROWS, tq), F32)],
        compiler_params=_params(("parallel", "parallel", "arbitrary")),
    )(qt4, z3, vt4, lq1, lk1, lq2, lk2, subln_g)


def _merge_kernel(x_ref, osgu_ref, oa_ref, g_ref, wg_ref, wba_ref, wbs_ref, wo_ref,
                  y_ref, h_ref, m_ref, *, tn):
    x = x_ref[...]
    h_ref[...] = _rms(x, g_ref[...]).astype(BF16)
    for j in range(D_MODEL // tn):
        cols = slice(j * tn, (j + 1) * tn)
        gate_s_cols = slice(D_MODEL + j * tn, D_MODEL + (j + 1) * tn)
        ga = jax.nn.sigmoid(jnp.dot(h_ref[...], wg_ref[:, cols], preferred_element_type=F32))
        gs = jax.nn.sigmoid(jnp.dot(h_ref[...], wg_ref[:, gate_s_cols],
                                    preferred_element_type=F32))
        ba = jnp.dot(oa_ref[...], wba_ref[:, cols], preferred_element_type=F32)
        bs = jnp.dot(osgu_ref[...], wbs_ref[:, cols], preferred_element_type=F32)
        m_ref[:, cols] = (ga * ba + gs * bs).astype(BF16)
    y_ref[...] = x + jnp.dot(m_ref[...], wo_ref[...], preferred_element_type=F32)


def _merge(x2, z, oa, norm_g, w_gates, wba, wbs, wo, tm, tn):
    T = x2.shape[0]
    whole = lambda a: pl.BlockSpec(a.shape, lambda i: (0,) * a.ndim)
    return pl.pallas_call(
        functools.partial(_merge_kernel, tn=tn),
        grid=(T // tm,),
        in_specs=[
            pl.BlockSpec((tm, D_MODEL), lambda i: (i, 0)),
            pl.BlockSpec((tm, SGU_W), lambda i: (i, ATTN_W // SGU_W)),
            pl.BlockSpec((tm, ATTN_W), lambda i: (i, 0)),
            whole(norm_g), whole(w_gates), whole(wba), whole(wbs), whole(wo),
        ],
        out_specs=pl.BlockSpec((tm, D_MODEL), lambda i: (i, 0)),
        out_shape=jax.ShapeDtypeStruct((T, D_MODEL), F32),
        scratch_shapes=[pltpu.VMEM((tm, D_MODEL), BF16), pltpu.VMEM((tm, D_MODEL), BF16)],
        compiler_params=_params(("parallel",)),
    )(x2, z, oa, norm_g, w_gates, wba, wbs, wo)


def _mlp_up_kernel(x_ref, g_ref, wu_ref, u_ref, h_ref, *, tf):
    h_ref[...] = _rms(x_ref[...], g_ref[...]).astype(BF16)
    for j in range(D_FF // tf):
        cols = slice(j * tf, (j + 1) * tf)
        u = jnp.dot(h_ref[...], wu_ref[:, cols], preferred_element_type=F32)
        u_ref[:, cols] = jnp.square(jnp.maximum(u, 0.0)).astype(BF16)


def _mlp_down_kernel(x_ref, u_ref, wd_ref, fg_ref, y_ref, *, final_norm):
    y = x_ref[...] + jnp.dot(u_ref[...], wd_ref[...], preferred_element_type=F32)
    if final_norm:
        y = _rms(y, fg_ref[...])
    y_ref[...] = y


def _mlp(x2, norm_g, wu, wd, final_g, final_norm, tm, tf):
    T = x2.shape[0]
    whole = lambda a: pl.BlockSpec(a.shape, lambda i: (0,) * a.ndim)
    rows = lambda width: pl.BlockSpec((tm, width), lambda i: (i, 0))
    u = pl.pallas_call(
        functools.partial(_mlp_up_kernel, tf=tf),
        grid=(T // tm,),
        in_specs=[rows(D_MODEL), whole(norm_g), whole(wu)],
        out_specs=rows(D_FF),
        out_shape=jax.ShapeDtypeStruct((T, D_FF), BF16),
        scratch_shapes=[pltpu.VMEM((tm, D_MODEL), BF16)],
        compiler_params=_params(("parallel",)),
    )(x2, norm_g, wu)
    return pl.pallas_call(
        functools.partial(_mlp_down_kernel, final_norm=final_norm),
        grid=(T // tm,),
        in_specs=[rows(D_MODEL), rows(D_FF), whole(wd), whole(final_g)],
        out_specs=rows(D_MODEL),
        out_shape=jax.ShapeDtypeStruct((T, D_MODEL), F32),
        compiler_params=_params(("parallel",)),
    )(x2, u, wd, final_g)


def _rotary_lane_order(w):
    split = ROT_DIM + DH // 2 - ROT_HALF
    return jnp.concatenate([w[..., :ROT_HALF], w[..., ROT_DIM:split],
                            w[..., ROT_HALF:ROT_DIM], w[..., split:]], axis=-1)


def _rot_tables(seq_len):
    pos = jnp.arange(seq_len, dtype=F32)
    inv_freq = 1.0 / (jnp.float32(ROPE_THETA) ** (jnp.arange(0, ROT_DIM, 2, dtype=F32) / ROT_DIM))
    ang = pos[:, None] * inv_freq[None, :]
    cos, sin = jnp.cos(ang), jnp.sin(ang)
    pad = DH // 2 - ROT_HALF
    ones, zeros = jnp.ones((seq_len, pad), F32), jnp.zeros((seq_len, pad), F32)
    rc = jnp.concatenate([cos, ones, cos, ones], axis=-1)
    rs = jnp.concatenate([-sin, zeros, sin, zeros], axis=-1)
    return rc * EXP2_SCALE, rs * EXP2_SCALE, rc, rs


def _token_tile(seq_len):
    return min(512, seq_len // 2)


def _prepare(p):
    row = lambda a: a.reshape(1, -1).astype(F32)
    layers = []
    for l in range(p["w_in"].shape[0]):
        w_in = p["w_in"][l]
        w_qk = _rotary_lane_order(w_in[:, :2 * ATTN_W].reshape(D_MODEL, -1, DH))
        w_proj = jnp.concatenate([w_qk.reshape(D_MODEL, 2 * ATTN_W),
                                  w_in[:, 2 * ATTN_W:PROJ_COLS]], axis=1).astype(BF16)
        layers.append(dict(
            attn_norm_g=row(p["attn_norm_g"][l]), w_proj=w_proj,
            w_gates=w_in[:, PROJ_COLS:].astype(BF16),
            lq1=row(p["lambda_q1"][l]), lk1=row(p["lambda_k1"][l]),
            lq2=row(p["lambda_q2"][l]), lk2=row(p["lambda_k2"][l]),
            subln_g=row(p["subln_g"][l]), ln_g=row(p["sgu_ln_g"][l]), ln_b=row(p["sgu_ln_b"][l]),
            sgu_w=p["sgu_w"][l].astype(BF16), sgu_bt=p["sgu_b"][l].T.astype(F32),
            w_br_attn=p["w_br_attn"][l].astype(BF16), w_br_sgu=p["w_br_sgu"][l].astype(BF16),
            w_out=p["w_out"][l].astype(BF16), mlp_norm_g=row(p["mlp_norm_g"][l]),
            w_up=p["w_up"][l].astype(BF16), w_down=p["w_down"][l].astype(BF16)))
    return layers, row(p["final_norm_g"])


def _trunk(x, layers, final_g):
    B, S, _ = x.shape
    tm = _token_tile(S)
    rot = _rot_tables(S)
    x2 = x.reshape(B * S, D_MODEL)
    for l, w in enumerate(layers):
        qt4, vt4, z = _in_proj(x2, B, S, w["attn_norm_g"], w["w_proj"], rot, w["ln_g"], w["ln_b"],
                               w["sgu_w"], w["sgu_bt"], tm)
        oa = _attention(qt4, z.reshape(B, S, ZCOLS), vt4, w["lq1"], w["lk1"], w["lq2"], w["lk2"],
                        w["subln_g"], _lambda_init(l))
        x2 = _merge(x2, z, oa.reshape(B * S, ATTN_W), w["attn_norm_g"], w["w_gates"],
                    w["w_br_attn"], w["w_br_sgu"], w["w_out"], MERGE_TM, MERGE_TN)
        x2 = _mlp(x2, w["mlp_norm_g"], w["w_up"], w["w_down"], final_g,
                  l == len(layers) - 1, MLP_TM, MLP_TF)
    return x2.reshape(B, S, D_MODEL)


def kernel(x_prompt, x_sample, attn_norm_g, w_in, lambda_q1, lambda_k1, lambda_q2, lambda_k2,
           subln_g, sgu_ln_g, sgu_ln_b, sgu_w, sgu_b, w_br_attn, w_br_sgu, w_out, mlp_norm_g,
           w_up, w_down, final_norm_g):
    p = dict(attn_norm_g=attn_norm_g, w_in=w_in, lambda_q1=lambda_q1, lambda_k1=lambda_k1,
             lambda_q2=lambda_q2, lambda_k2=lambda_k2, subln_g=subln_g, sgu_ln_g=sgu_ln_g,
             sgu_ln_b=sgu_ln_b, sgu_w=sgu_w, sgu_b=sgu_b, w_br_attn=w_br_attn, w_br_sgu=w_br_sgu,
             w_out=w_out, mlp_norm_g=mlp_norm_g, w_up=w_up, w_down=w_down,
             final_norm_g=final_norm_g)
    layers, final_g = _prepare(p)
    return _trunk(x_prompt, layers, final_g), _trunk(x_sample, layers, final_g)
```

```python
import functools
import math

import jax
import jax.numpy as jnp
from jax import lax
from jax.experimental import pallas as pl
from jax.experimental.pallas import tpu as pltpu

D_MODEL = 2048
N_HEADS = 4
DH = 128
HEAD_W = 2 * DH
ATTN_W = N_HEADS * HEAD_W
ONES_ROWS = 16
VT_ROWS = HEAD_W + ONES_ROWS
ROT_DIM = DH // 4
ROT_HALF = ROT_DIM // 2
ROPE_THETA = 500000.0
ATTN_SCALE = DH ** -0.5
CHUNK = 128
SGU_GROUPS = 8
SGU_W = SGU_GROUPS * CHUNK
D_FF = 4 * D_MODEL
PROJ_COLS = 3 * ATTN_W + 2 * SGU_W
ZCOLS = ATTN_W + SGU_W
RMS_EPS = 1e-6
LN_EPS = 1e-5
SQRT_HALF = math.sqrt(0.5)
EXP2_SCALE = ATTN_SCALE * math.log2(math.e)

VMEM_LIMIT_BYTES_V7X = 56 * 1024 * 1024

MERGE_TM = 256
MERGE_TN = 512
MLP_TM = 256
MLP_TF = 1024

F32 = jnp.float32
BF16 = jnp.bfloat16


def _lambda_init(layer_idx):
    return 0.8 - 0.6 * math.exp(-0.3 * layer_idx)


def _params(semantics):
    return pltpu.CompilerParams(dimension_semantics=semantics,
                                vmem_limit_bytes=VMEM_LIMIT_BYTES_V7X)


def _rms(x, g):
    ms = jnp.mean(x * x, axis=-1, keepdims=True)
    return x * lax.rsqrt(ms + RMS_EPS) * g


def _gelu(x):
    return 0.5 * x * (1.0 + lax.erf(x * SQRT_HALF))


def _in_proj_kernel(x_ref, g_ref, w_ref, rcq_ref, rsq_ref, rck_ref, rsk_ref, lng_ref, lnb_ref,
                    sw_ref, sbt_ref, qt_ref, vt_ref, z_ref, h_ref, su_ref, svn_ref):
    tm = x_ref.shape[0]
    h_ref[...] = _rms(x_ref[...], g_ref[...]).astype(BF16)

    def proj(c):
        return jnp.dot(h_ref[...], w_ref[:, c * ATTN_W:(c + 1) * ATTN_W],
                       preferred_element_type=F32)

    def rotary(acc, rc, rs, u):
        t = acc[:, u * DH:(u + 1) * DH]
        return t * rc + pltpu.roll(t, DH // 2, 1) * rs

    y = _gelu(proj(4))
    mu = jnp.mean(y, axis=-1, keepdims=True)
    yc = y - mu
    var = jnp.mean(yc * yc, axis=-1, keepdims=True)
    svn_ref[...] = (yc * lax.rsqrt(var + LN_EPS) * lng_ref[...] + lnb_ref[...]).astype(BF16)
    su_ref[...] = _gelu(proj(3))

    for r in range(tm // CHUNK):
        rows = slice(r * CHUNK, (r + 1) * CHUNK)
        for g in range(SGU_GROUPS):
            cols = slice(g * CHUNK, (g + 1) * CHUNK)
            mixed = jnp.dot(sw_ref[g], svn_ref[rows, cols], preferred_element_type=F32)
            mixed = mixed + sbt_ref[:, g:g + 1]
            out_cols = slice(ATTN_W + g * CHUNK, ATTN_W + (g + 1) * CHUNK)
            z_ref[rows, out_cols] = (su_ref[rows, cols] * mixed).astype(BF16)

    acc = proj(0)
    rc, rs = rcq_ref[...], rsq_ref[...]
    for u in range(ATTN_W // DH):
        qt_ref[u * DH:(u + 1) * DH, :] = rotary(acc, rc, rs, u).T.astype(BF16)

    acc = proj(1)
    rc, rs = rck_ref[...], rsk_ref[...]
    for u in range(ATTN_W // DH):
        z_ref[:, u * DH:(u + 1) * DH] = rotary(acc, rc, rs, u).astype(BF16)

    vt = proj(2).T.astype(BF16)
    for hd in range(N_HEADS):
        vt_ref[hd * VT_ROWS:hd * VT_ROWS + HEAD_W, :] = vt[hd * HEAD_W:(hd + 1) * HEAD_W, :]
        vt_ref[hd * VT_ROWS + HEAD_W:(hd + 1) * VT_ROWS, :] = jnp.ones((ONES_ROWS, tm), BF16)


def _in_proj(x2, batch, seq_len, norm_g, w_proj, rot, ln_g, ln_b, sgu_w_bf, sgu_bt, tm):
    T = x2.shape[0]
    n_pos_blocks = seq_len // tm
    rot_spec = pl.BlockSpec((tm, DH), lambda i: (i % n_pos_blocks, 0))
    whole = lambda a: pl.BlockSpec(a.shape, lambda i: (0,) * a.ndim)
    t_spec = lambda rows: pl.BlockSpec((None, None, rows, tm),
                                       lambda i: (i // n_pos_blocks, i % n_pos_blocks, 0, 0))
    t_shape = lambda rows: jax.ShapeDtypeStruct((batch, n_pos_blocks, rows, tm), BF16)
    return pl.pallas_call(
        _in_proj_kernel,
        grid=(T // tm,),
        in_specs=[
            pl.BlockSpec((tm, D_MODEL), lambda i: (i, 0)),
            whole(norm_g), whole(w_proj),
            rot_spec, rot_spec, rot_spec, rot_spec,
            whole(ln_g), whole(ln_b), whole(sgu_w_bf), whole(sgu_bt),
        ],
        out_specs=[t_spec(ATTN_W), t_spec(N_HEADS * VT_ROWS),
                   pl.BlockSpec((tm, ZCOLS), lambda i: (i, 0))],
        out_shape=[t_shape(ATTN_W), t_shape(N_HEADS * VT_ROWS),
                   jax.ShapeDtypeStruct((T, ZCOLS), BF16)],
        scratch_shapes=[pltpu.VMEM((tm, D_MODEL), BF16),
                        pltpu.VMEM((tm, SGU_W), F32),
                        pltpu.VMEM((tm, SGU_W), BF16)],
        compiler_params=_params(("parallel",)),
    )(x2, norm_g, w_proj, rot[0], rot[1], rot[2], rot[3], ln_g, ln_b, sgu_w_bf, sgu_bt)


def _attn_kernel(qt_ref, k_ref, vt_ref, lq1_ref, lk1_ref, lq2_ref, lk2_ref, g_ref, o_ref,
                 s0_ref, s1_ref, p0_ref, p1_ref, acc_ref, *, tk, n_kv, lambda_init):
    s_refs, p_refs = (s0_ref, s1_ref), (p0_ref, p1_ref)

    def scores(j, slot):
        rows = pl.ds(pl.multiple_of(j * tk, tk), tk)
        col_max = []
        for c in range(2):
            s = jnp.dot(k_ref[rows, c * DH:(c + 1) * DH], qt_ref[c * DH:(c + 1) * DH, :],
                        preferred_element_type=F32)
            s_refs[slot][c] = s
            col_max.append(jnp.max(s, axis=0, keepdims=True))
        return tuple(col_max)

    def softmax(slot, col_max, run_max):
        new_max, alphas = [], []
        for c in range(2):
            m_new = jnp.maximum(run_max[c], col_max[c])
            alphas.append(jnp.exp2(run_max[c] - m_new))
            p_refs[slot][c] = jnp.exp2(s_refs[slot][c] - m_new).astype(BF16)
            new_max.append(m_new)
        return tuple(new_max), tuple(alphas)

    def values(j, slot, alphas):
        vt = vt_ref[j]
        for c in range(2):
            pv = jnp.dot(vt, p_refs[slot][c], preferred_element_type=F32)
            acc_ref[c] = alphas[c] * acc_ref[c] + pv

    tq = qt_ref.shape[1]
    acc_ref[...] = jnp.zeros(acc_ref.shape, F32)
    run_max = tuple(jnp.full((1, tq), -jnp.inf, F32) for _ in range(2))

    col_max = scores(0, 0)
    col_max_next = scores(1, 1)
    run_max, alphas = softmax(0, col_max, run_max)

    def two_chunks(u, carry):
        col_max, run_max, alphas = carry
        col_max_a = scores(2 * u + 2, 0)
        run_max, alphas_a = softmax(1, col_max, run_max)
        values(2 * u, 0, alphas)
        col_max_b = scores(2 * u + 3, 1)
        run_max, alphas_b = softmax(0, col_max_a, run_max)
        values(2 * u + 1, 1, alphas_a)
        return col_max_b, run_max, alphas_b

    col_max, run_max, alphas = lax.fori_loop(0, n_kv // 2 - 1, two_chunks,
                                             (col_max_next, run_max, alphas))
    _, alphas_last = softmax(1, col_max, run_max)
    values(n_kv - 2, 0, alphas)
    values(n_kv - 1, 1, alphas_last)

    l1, l2 = acc_ref[0, HEAD_W:HEAD_W + 1, :], acc_ref[1, HEAD_W:HEAD_W + 1, :]
    lam = (jnp.exp(jnp.sum(lq1_ref[...] * lk1_ref[...], axis=-1, keepdims=True))
           - jnp.exp(jnp.sum(lq2_ref[...] * lk2_ref[...], axis=-1, keepdims=True))
           + lambda_init)
    o_t = (acc_ref[0, :HEAD_W, :] * (1.0 / l1)
           - lam * (acc_ref[1, :HEAD_W, :] * (1.0 / l2)))
    ms = jnp.mean(o_t * o_t, axis=0, keepdims=True)
    y = (o_t * lax.rsqrt(ms + RMS_EPS)).T
    o_ref[...] = (y * g_ref[...] * (1.0 - lambda_init)).astype(BF16)


def _attention(qt4, z3, vt4, lq1, lk1, lq2, lk2, subln_g, lambda_init):
    B, n_kv, _, tk = vt4.shape
    n_q, tq = qt4.shape[1], qt4.shape[3]
    S = z3.shape[1]
    assert n_kv >= 2 and n_kv % 2 == 0, "the chunk pipeline takes an even number of key chunks"
    vec = lambda n: pl.BlockSpec((1, n), lambda b, h, i: (0, 0))
    kern = functools.partial(_attn_kernel, tk=tk, n_kv=n_kv, lambda_init=lambda_init)
    slot = lambda dt: pltpu.VMEM((2, tk, tq), dt)
    return pl.pallas_call(
        kern,
        grid=(B, N_HEADS, n_q),
        in_specs=[
            pl.BlockSpec((None, None, HEAD_W, tq), lambda b, h, i: (b, i, h, 0)),
            pl.BlockSpec((None, S, HEAD_W), lambda b, h, i: (b, 0, h)),
            pl.BlockSpec((None, n_kv, VT_ROWS, tk), lambda b, h, i: (b, 0, h, 0)),
            vec(DH), vec(DH), vec(DH), vec(DH),
            vec(HEAD_W),
        ],
        out_specs=pl.BlockSpec((None, tq, HEAD_W), lambda b, h, i: (b, i, h)),
        out_shape=jax.ShapeDtypeStruct((B, S, ATTN_W), BF16),
        scratch_shapes=[slot(F32), slot(F32), slot(BF16), slot(BF16),
                        pltpu.VMEM((2, VT_ROWS, tq), F32)],
        compiler_params=_params(("parallel", "parallel", "arbitrary")),
    )(qt4, z3, vt4, lq1, lk1, lq2, lk2, subln_g)


def _merge_kernel(x_ref, osgu_ref, oa_ref, g_ref, wg_ref, wba_ref, wbs_ref, wo_ref,
                  y_ref, h_ref, m_ref, *, tn):
    x = x_ref[...]
    h_ref[...] = _rms(x, g_ref[...]).astype(BF16)
    for j in range(D_MODEL // tn):
        cols = slice(j * tn, (j + 1) * tn)
        gate_s_cols = slice(D_MODEL + j * tn, D_MODEL + (j + 1) * tn)
        ga = jax.nn.sigmoid(jnp.dot(h_ref[...], wg_ref[:, cols], preferred_element_type=F32))
        gs = jax.nn.sigmoid(jnp.dot(h_ref[...], wg_ref[:, gate_s_cols],
                                    preferred_element_type=F32))
        ba = jnp.dot(oa_ref[...], wba_ref[:, cols], preferred_element_type=F32)
        bs = jnp.dot(osgu_ref[...], wbs_ref[:, cols], preferred_element_type=F32)
        m_ref[:, cols] = (ga * ba + gs * bs).astype(BF16)
    y_ref[...] = x + jnp.dot(m_ref[...], wo_ref[...], preferred_element_type=F32)


def _merge(x2, z, oa, norm_g, w_gates, wba, wbs, wo, tm, tn):
    T = x2.shape[0]
    whole = lambda a: pl.BlockSpec(a.shape, lambda i: (0,) * a.ndim)
    return pl.pallas_call(
        functools.partial(_merge_kernel, tn=tn),
        grid=(T // tm,),
        in_specs=[
            pl.BlockSpec((tm, D_MODEL), lambda i: (i, 0)),
            pl.BlockSpec((tm, SGU_W), lambda i: (i, ATTN_W // SGU_W)),
            pl.BlockSpec((tm, ATTN_W), lambda i: (i, 0)),
            whole(norm_g), whole(w_gates), whole(wba), whole(wbs), whole(wo),
        ],
        out_specs=pl.BlockSpec((tm, D_MODEL), lambda i: (i, 0)),
        out_shape=jax.ShapeDtypeStruct((T, D_MODEL), F32),
        scratch_shapes=[pltpu.VMEM((tm, D_MODEL), BF16), pltpu.VMEM((tm, D_MODEL), BF16)],
        compiler_params=_params(("parallel",)),
    )(x2, z, oa, norm_g, w_gates, wba, wbs, wo)


def _mlp_up_kernel(x_ref, g_ref, wu_ref, u_ref, h_ref, *, tf):
    h_ref[...] = _rms(x_ref[...], g_ref[...]).astype(BF16)
    for j in range(D_FF // tf):
        cols = slice(j * tf, (j + 1) * tf)
        u = jnp.dot(h_ref[...], wu_ref[:, cols], preferred_element_type=F32)
        u_ref[:, cols] = jnp.square(jnp.maximum(u, 0.0)).astype(BF16)


def _mlp_down_kernel(x_ref, u_ref, wd_ref, fg_ref, y_ref, *, final_norm):
    y = x_ref[...] + jnp.dot(u_ref[...], wd_ref[...], preferred_element_type=F32)
    if final_norm:
        y = _rms(y, fg_ref[...])
    y_ref[...] = y


def _mlp(x2, norm_g, wu, wd, final_g, final_norm, tm, tf):
    T = x2.shape[0]
    whole = lambda a: pl.BlockSpec(a.shape, lambda i: (0,) * a.ndim)
    rows = lambda width: pl.BlockSpec((tm, width), lambda i: (i, 0))
    u = pl.pallas_call(
        functools.partial(_mlp_up_kernel, tf=tf),
        grid=(T // tm,),
        in_specs=[rows(D_MODEL), whole(norm_g), whole(wu)],
        out_specs=rows(D_FF),
        out_shape=jax.ShapeDtypeStruct((T, D_FF), BF16),
        scratch_shapes=[pltpu.VMEM((tm, D_MODEL), BF16)],
        compiler_params=_params(("parallel",)),
    )(x2, norm_g, wu)
    return pl.pallas_call(
        functools.partial(_mlp_down_kernel, final_norm=final_norm),
        grid=(T // tm,),
        in_specs=[rows(D_MODEL), rows(D_FF), whole(wd), whole(final_g)],
        out_specs=rows(D_MODEL),
        out_shape=jax.ShapeDtypeStruct((T, D_MODEL), F32),
        compiler_params=_params(("parallel",)),
    )(x2, u, wd, final_g)


def _rotary_lane_order(w):
    split = ROT_DIM + DH // 2 - ROT_HALF
    return jnp.concatenate([w[..., :ROT_HALF], w[..., ROT_DIM:split],
                            w[..., ROT_HALF:ROT_DIM], w[..., split:]], axis=-1)


def _rot_tables(seq_len):
    pos = jnp.arange(seq_len, dtype=F32)
    inv_freq = 1.0 / (jnp.float32(ROPE_THETA) ** (jnp.arange(0, ROT_DIM, 2, dtype=F32) / ROT_DIM))
    ang = pos[:, None] * inv_freq[None, :]
    cos, sin = jnp.cos(ang), jnp.sin(ang)
    pad = DH // 2 - ROT_HALF
    ones, zeros = jnp.ones((seq_len, pad), F32), jnp.zeros((seq_len, pad), F32)
    rc = jnp.concatenate([cos, ones, cos, ones], axis=-1)
    rs = jnp.concatenate([-sin, zeros, sin, zeros], axis=-1)
    return rc * EXP2_SCALE, rs * EXP2_SCALE, rc, rs


def _token_tile(seq_len):
    return min(512, seq_len // 2)


def _prepare(p):
    row = lambda a: a.reshape(1, -1).astype(F32)
    layers = []
    for l in range(p["w_in"].shape[0]):
        w_in = p["w_in"][l]
        w_qk = _rotary_lane_order(w_in[:, :2 * ATTN_W].reshape(D_MODEL, -1, DH))
        w_proj = jnp.concatenate([w_qk.reshape(D_MODEL, 2 * ATTN_W),
                                  w_in[:, 2 * ATTN_W:PROJ_COLS]], axis=1).astype(BF16)
        layers.append(dict(
            attn_norm_g=row(p["attn_norm_g"][l]), w_proj=w_proj,
            w_gates=w_in[:, PROJ_COLS:].astype(BF16),
            lq1=row(p["lambda_q1"][l]), lk1=row(p["lambda_k1"][l]),
            lq2=row(p["lambda_q2"][l]), lk2=row(p["lambda_k2"][l]),
            subln_g=row(p["subln_g"][l]), ln_g=row(p["sgu_ln_g"][l]), ln_b=row(p["sgu_ln_b"][l]),
            sgu_w=p["sgu_w"][l].astype(BF16), sgu_bt=p["sgu_b"][l].T.astype(F32),
            w_br_attn=p["w_br_attn"][l].astype(BF16), w_br_sgu=p["w_br_sgu"][l].astype(BF16),
            w_out=p["w_out"][l].astype(BF16), mlp_norm_g=row(p["mlp_norm_g"][l]),
            w_up=p["w_up"][l].astype(BF16), w_down=p["w_down"][l].astype(BF16)))
    return layers, row(p["final_norm_g"])


def _trunk(x, layers, final_g):
    B, S, _ = x.shape
    tm = _token_tile(S)
    rot = _rot_tables(S)
    x2 = x.reshape(B * S, D_MODEL)
    for l, w in enumerate(layers):
        qt4, vt4, z = _in_proj(x2, B, S, w["attn_norm_g"], w["w_proj"], rot, w["ln_g"], w["ln_b"],
                               w["sgu_w"], w["sgu_bt"], tm)
        oa = _attention(qt4, z.reshape(B, S, ZCOLS), vt4, w["lq1"], w["lk1"], w["lq2"], w["lk2"],
                        w["subln_g"], _lambda_init(l))
        x2 = _merge(x2, z, oa.reshape(B * S, ATTN_W), w["attn_norm_g"], w["w_gates"],
                    w["w_br_attn"], w["w_br_sgu"], w["w_out"], MERGE_TM, MERGE_TN)
        x2 = _mlp(x2, w["mlp_norm_g"], w["w_up"], w["w_down"], final_g,
                  l == len(layers) - 1, MLP_TM, MLP_TF)
    return x2.reshape(B, S, D_MODEL)


def kernel(x_prompt, x_sample, attn_norm_g, w_in, lambda_q1, lambda_k1, lambda_q2, lambda_k2,
           subln_g, sgu_ln_g, sgu_ln_b, sgu_w, sgu_b, w_br_attn, w_br_sgu, w_out, mlp_norm_g,
           w_up, w_down, final_norm_g):
    p = dict(attn_norm_g=attn_norm_g, w_in=w_in, lambda_q1=lambda_q1, lambda_k1=lambda_k1,
             lambda_q2=lambda_q2, lambda_k2=lambda_k2, subln_g=subln_g, sgu_ln_g=sgu_ln_g,
             sgu_ln_b=sgu_ln_b, sgu_w=sgu_w, sgu_b=sgu_b, w_br_attn=w_br_attn, w_br_sgu=w_br_sgu,
             w_out=w_out, mlp_norm_g=mlp_norm_g, w_up=w_up, w_down=w_down,
             final_norm_g=final_norm_g)
    layers, final_g = _prepare(p)
    return _trunk(x_prompt, layers, final_g), _trunk(x_sample, layers, final_g)
```

```python
import functools
import math

import jax
import jax.numpy as jnp
from jax import lax
from jax.experimental import pallas as pl
from jax.experimental.pallas import tpu as pltpu

D_MODEL = 2048
N_HEADS = 4
DH = 128
HEAD_W = 2 * DH
ATTN_W = N_HEADS * HEAD_W
ONES_ROWS = 16
VT_ROWS = HEAD_W + ONES_ROWS
ROT_DIM = DH // 4
ROT_HALF = ROT_DIM // 2
ROPE_THETA = 500000.0
ATTN_SCALE = DH ** -0.5
CHUNK = 128
SGU_GROUPS = 8
SGU_W = SGU_GROUPS * CHUNK
D_FF = 4 * D_MODEL
PROJ_COLS = 3 * ATTN_W + 2 * SGU_W
ZCOLS = ATTN_W + SGU_W
RMS_EPS = 1e-6
LN_EPS = 1e-5
SQRT_HALF = math.sqrt(0.5)
EXP2_SCALE = ATTN_SCALE * math.log2(math.e)

VMEM_LIMIT_BYTES_V7X = 56 * 1024 * 1024

ATTN_SHORT_CHUNKS = 4
MERGE_TM = 256
MERGE_TN = 512
MLP_TM = 256
MLP_TF = 1024

F32 = jnp.float32
BF16 = jnp.bfloat16


def _lambda_init(layer_idx):
    return 0.8 - 0.6 * math.exp(-0.3 * layer_idx)


def _params(semantics):
    return pltpu.CompilerParams(dimension_semantics=semantics,
                                vmem_limit_bytes=VMEM_LIMIT_BYTES_V7X)


def _rms(x, g):
    ms = jnp.mean(x * x, axis=-1, keepdims=True)
    return x * lax.rsqrt(ms + RMS_EPS) * g


def _gelu(x):
    return 0.5 * x * (1.0 + lax.erf(x * SQRT_HALF))


def _in_proj_kernel(x_ref, g_ref, w_ref, rcq_ref, rsq_ref, rck_ref, rsk_ref, lng_ref, lnb_ref,
                    sw_ref, sbt_ref, qt_ref, vt_ref, z_ref, h_ref, su_ref, svn_ref):
    tm = x_ref.shape[0]
    h_ref[...] = _rms(x_ref[...], g_ref[...]).astype(BF16)

    def proj(c):
        return jnp.dot(h_ref[...], w_ref[:, c * ATTN_W:(c + 1) * ATTN_W],
                       preferred_element_type=F32)

    def rotary(acc, rc, rs, u):
        t = acc[:, u * DH:(u + 1) * DH]
        return t * rc + pltpu.roll(t, DH // 2, 1) * rs

    y = _gelu(proj(4))
    mu = jnp.mean(y, axis=-1, keepdims=True)
    yc = y - mu
    var = jnp.mean(yc * yc, axis=-1, keepdims=True)
    svn_ref[...] = (yc * lax.rsqrt(var + LN_EPS) * lng_ref[...] + lnb_ref[...]).astype(BF16)
    su_ref[...] = _gelu(proj(3))

    for r in range(tm // CHUNK):
        rows = slice(r * CHUNK, (r + 1) * CHUNK)
        for g in range(SGU_GROUPS):
            cols = slice(g * CHUNK, (g + 1) * CHUNK)
            mixed = jnp.dot(sw_ref[g], svn_ref[rows, cols], preferred_element_type=F32)
            mixed = mixed + sbt_ref[:, g:g + 1]
            out_cols = slice(ATTN_W + g * CHUNK, ATTN_W + (g + 1) * CHUNK)
            z_ref[rows, out_cols] = (su_ref[rows, cols] * mixed).astype(BF16)

    acc = proj(0)
    rc, rs = rcq_ref[...], rsq_ref[...]
    for u in range(ATTN_W // DH):
        qt_ref[u * DH:(u + 1) * DH, :] = rotary(acc, rc, rs, u).T.astype(BF16)

    acc = proj(1)
    rc, rs = rck_ref[...], rsk_ref[...]
    for u in range(ATTN_W // DH):
        z_ref[:, u * DH:(u + 1) * DH] = rotary(acc, rc, rs, u).astype(BF16)

    vt = proj(2).T.astype(BF16)
    for hd in range(N_HEADS):
        vt_ref[hd * VT_ROWS:hd * VT_ROWS + HEAD_W, :] = vt[hd * HEAD_W:(hd + 1) * HEAD_W, :]
        vt_ref[hd * VT_ROWS + HEAD_W:(hd + 1) * VT_ROWS, :] = jnp.ones((ONES_ROWS, tm), BF16)


def _in_proj(x2, batch, seq_len, norm_g, w_proj, rot, ln_g, ln_b, sgu_w_bf, sgu_bt, tm):
    T = x2.shape[0]
    n_pos_blocks = seq_len // tm
    rot_spec = pl.BlockSpec((tm, DH), lambda i: (i % n_pos_blocks, 0))
    whole = lambda a: pl.BlockSpec(a.shape, lambda i: (0,) * a.ndim)
    t_spec = lambda rows: pl.BlockSpec((None, None, rows, tm),
                                       lambda i: (i // n_pos_blocks, i % n_pos_blocks, 0, 0))
    t_shape = lambda rows: jax.ShapeDtypeStruct((batch, n_pos_blocks, rows, tm), BF16)
    return pl.pallas_call(
        _in_proj_kernel,
        grid=(T // tm,),
        in_specs=[
            pl.BlockSpec((tm, D_MODEL), lambda i: (i, 0)),
            whole(norm_g), whole(w_proj),
            rot_spec, rot_spec, rot_spec, rot_spec,
            whole(ln_g), whole(ln_b), whole(sgu_w_bf), whole(sgu_bt),
        ],
        out_specs=[t_spec(ATTN_W), t_spec(N_HEADS * VT_ROWS),
                   pl.BlockSpec((tm, ZCOLS), lambda i: (i, 0))],
        out_shape=[t_shape(ATTN_W), t_shape(N_HEADS * VT_ROWS),
                   jax.ShapeDtypeStruct((T, ZCOLS), BF16)],
        scratch_shapes=[pltpu.VMEM((tm, D_MODEL), BF16),
                        pltpu.VMEM((tm, SGU_W), F32),
                        pltpu.VMEM((tm, SGU_W), BF16)],
        compiler_params=_params(("parallel",)),
    )(x2, norm_g, w_proj, rot[0], rot[1], rot[2], rot[3], ln_g, ln_b, sgu_w_bf, sgu_bt)


def _attn_kernel(qt_ref, k_ref, vt_ref, lq1_ref, lk1_ref, lq2_ref, lk2_ref, g_ref, o_ref,
                 s0_ref, s1_ref, p0_ref, p1_ref, acc_ref, *, tk, n_kv, lambda_init):
    s_refs, p_refs = (s0_ref, s1_ref), (p0_ref, p1_ref)

    def scores(j, slot):
        rows = pl.ds(pl.multiple_of(j * tk, tk), tk)
        col_max = []
        for c in range(2):
            s = jnp.dot(k_ref[rows, c * DH:(c + 1) * DH], qt_ref[c * DH:(c + 1) * DH, :],
                        preferred_element_type=F32)
            s_refs[slot][c] = s
            col_max.append(jnp.max(s, axis=0, keepdims=True))
        return tuple(col_max)

    def softmax(slot, col_max, run_max):
        new_max, alphas = [], []
        for c in range(2):
            m_new = jnp.maximum(run_max[c], col_max[c])
            alphas.append(jnp.exp2(run_max[c] - m_new))
            p_refs[slot][c] = jnp.exp2(s_refs[slot][c] - m_new).astype(BF16)
            new_max.append(m_new)
        return tuple(new_max), tuple(alphas)

    def values(j, slot, alphas):
        vt = vt_ref[j]
        for c in range(2):
            pv = jnp.dot(vt, p_refs[slot][c], preferred_element_type=F32)
            acc_ref[c] = alphas[c] * acc_ref[c] + pv

    tq = qt_ref.shape[1]
    acc_ref[...] = jnp.zeros(acc_ref.shape, F32)
    run_max = tuple(jnp.full((1, tq), -jnp.inf, F32) for _ in range(2))

    col_max = scores(0, 0)
    col_max_next = scores(1, 1)
    run_max, alphas = softmax(0, col_max, run_max)

    def two_chunks(u, carry):
        col_max, run_max, alphas = carry
        col_max_a = scores(2 * u + 2, 0)
        run_max, alphas_a = softmax(1, col_max, run_max)
        values(2 * u, 0, alphas)
        col_max_b = scores(2 * u + 3, 1)
        run_max, alphas_b = softmax(0, col_max_a, run_max)
        values(2 * u + 1, 1, alphas_a)
        return col_max_b, run_max, alphas_b

    col_max, run_max, alphas = lax.fori_loop(0, n_kv // 2 - 1, two_chunks,
                                             (col_max_next, run_max, alphas))
    _, alphas_last = softmax(1, col_max, run_max)
    values(n_kv - 2, 0, alphas)
    values(n_kv - 1, 1, alphas_last)

    l1, l2 = acc_ref[0, HEAD_W:HEAD_W + 1, :], acc_ref[1, HEAD_W:HEAD_W + 1, :]
    lam = (jnp.exp(jnp.sum(lq1_ref[...] * lk1_ref[...], axis=-1, keepdims=True))
           - jnp.exp(jnp.sum(lq2_ref[...] * lk2_ref[...], axis=-1, keepdims=True))
           + lambda_init)
    o_t = (acc_ref[0, :HEAD_W, :] * (1.0 / l1)
           - lam * (acc_ref[1, :HEAD_W, :] * (1.0 / l2)))
    ms = jnp.mean(o_t * o_t, axis=0, keepdims=True)
    y = (o_t * lax.rsqrt(ms + RMS_EPS)).T
    o_ref[...] = (y * g_ref[...] * (1.0 - lambda_init)).astype(BF16)


def _attn_short_kernel(qt_ref, k_ref, vt_ref, lq1_ref, lk1_ref, lq2_ref, lk2_ref, g_ref, o_ref,
                       s_ref, p_ref, *, tk, n_kv, lambda_init):
    for c in range(2):
        s_ref[c] = jnp.dot(k_ref[:, c * DH:(c + 1) * DH], qt_ref[c * DH:(c + 1) * DH, :],
                           preferred_element_type=F32)
    accs = []
    for c in range(2):
        s = s_ref[c]
        p_ref[c] = jnp.exp2(s - jnp.max(s, axis=0, keepdims=True)).astype(BF16)
        acc = jnp.dot(vt_ref[0], p_ref[c, 0:tk, :], preferred_element_type=F32)
        for j in range(1, n_kv):
            acc = acc + jnp.dot(vt_ref[j], p_ref[c, j * tk:(j + 1) * tk, :],
                                preferred_element_type=F32)
        accs.append(acc)

    l1, l2 = accs[0][HEAD_W:HEAD_W + 1, :], accs[1][HEAD_W:HEAD_W + 1, :]
    lam = (jnp.exp(jnp.sum(lq1_ref[...] * lk1_ref[...], axis=-1, keepdims=True))
           - jnp.exp(jnp.sum(lq2_ref[...] * lk2_ref[...], axis=-1, keepdims=True))
           + lambda_init)
    o_t = accs[0][:HEAD_W, :] * (1.0 / l1) - lam * (accs[1][:HEAD_W, :] * (1.0 / l2))
    ms = jnp.mean(o_t * o_t, axis=0, keepdims=True)
    y = (o_t * lax.rsqrt(ms + RMS_EPS)).T
    o_ref[...] = (y * g_ref[...] * (1.0 - lambda_init)).astype(BF16)


def _attention(qt4, z3, vt4, lq1, lk1, lq2, lk2, subln_g, lambda_init):
    B, n_kv, _, tk = vt4.shape
    n_q, tq = qt4.shape[1], qt4.shape[3]
    S = z3.shape[1]
    assert n_kv >= 2 and n_kv % 2 == 0, "the chunk pipeline takes an even number of key chunks"
    vec = lambda n: pl.BlockSpec((1, n), lambda b, h, i: (0, 0))
    short = n_kv <= ATTN_SHORT_CHUNKS
    kern = functools.partial(_attn_short_kernel if short else _attn_kernel,
                             tk=tk, n_kv=n_kv, lambda_init=lambda_init)
    slot = lambda dt: pltpu.VMEM((2, tk, tq), dt)
    whole = lambda dt: pltpu.VMEM((2, S, tq), dt)
    scratch = ([whole(F32), whole(BF16)] if short else
               [slot(F32), slot(F32), slot(BF16), slot(BF16), pltpu.VMEM((2, VT_ROWS, tq), F32)])
    return pl.pallas_call(
        kern,
        grid=(B, N_HEADS, n_q),
        in_specs=[
            pl.BlockSpec((None, None, HEAD_W, tq), lambda b, h, i: (b, i, h, 0)),
            pl.BlockSpec((None, S, HEAD_W), lambda b, h, i: (b, 0, h)),
            pl.BlockSpec((None, n_kv, VT_ROWS, tk), lambda b, h, i: (b, 0, h, 0)),
            vec(DH), vec(DH), vec(DH), vec(DH),
            vec(HEAD_W),
        ],
        out_specs=pl.BlockSpec((None, tq, HEAD_W), lambda b, h, i: (b, i, h)),
        out_shape=jax.ShapeDtypeStruct((B, S, ATTN_W), BF16),
        scratch_shapes=scratch,
        compiler_params=_params(("parallel", "parallel", "arbitrary")),
    )(qt4, z3, vt4, lq1, lk1, lq2, lk2, subln_g)


def _merge_kernel(x_ref, osgu_ref, oa_ref, g_ref, wg_ref, wba_ref, wbs_ref, wo_ref,
                  y_ref, h_ref, m_ref, *, tn):
    x = x_ref[...]
    h_ref[...] = _rms(x, g_ref[...]).astype(BF16)
    for j in range(D_MODEL // tn):
        cols = slice(j * tn, (j + 1) * tn)
        gate_s_cols = slice(D_MODEL + j * tn, D_MODEL + (j + 1) * tn)
        ga = jax.nn.sigmoid(jnp.dot(h_ref[...], wg_ref[:, cols], preferred_element_type=F32))
        gs = jax.nn.sigmoid(jnp.dot(h_ref[...], wg_ref[:, gate_s_cols],
                                    preferred_element_type=F32))
        ba = jnp.dot(oa_ref[...], wba_ref[:, cols], preferred_element_type=F32)
        bs = jnp.dot(osgu_ref[...], wbs_ref[:, cols], preferred_element_type=F32)
        m_ref[:, cols] = (ga * ba + gs * bs).astype(BF16)
    y_ref[...] = x + jnp.dot(m_ref[...], wo_ref[...], preferred_element_type=F32)


def _merge(x2, z, oa, norm_g, w_gates, wba, wbs, wo, tm, tn):
    T = x2.shape[0]
    whole = lambda a: pl.BlockSpec(a.shape, lambda i: (0,) * a.ndim)
    return pl.pallas_call(
        functools.partial(_merge_kernel, tn=tn),
        grid=(T // tm,),
        in_specs=[
            pl.BlockSpec((tm, D_MODEL), lambda i: (i, 0)),
            pl.BlockSpec((tm, SGU_W), lambda i: (i, ATTN_W // SGU_W)),
            pl.BlockSpec((tm, ATTN_W), lambda i: (i, 0)),
            whole(norm_g), whole(w_gates), whole(wba), whole(wbs), whole(wo),
        ],
        out_specs=pl.BlockSpec((tm, D_MODEL), lambda i: (i, 0)),
        out_shape=jax.ShapeDtypeStruct((T, D_MODEL), F32),
        scratch_shapes=[pltpu.VMEM((tm, D_MODEL), BF16), pltpu.VMEM((tm, D_MODEL), BF16)],
        compiler_params=_params(("parallel",)),
    )(x2, z, oa, norm_g, w_gates, wba, wbs, wo)


def _mlp_up_kernel(x_ref, g_ref, wu_ref, u_ref, h_ref, *, tf):
    h_ref[...] = _rms(x_ref[...], g_ref[...]).astype(BF16)
    for j in range(D_FF // tf):
        cols = slice(j * tf, (j + 1) * tf)
        u = jnp.dot(h_ref[...], wu_ref[:, cols], preferred_element_type=F32)
        u_ref[:, cols] = jnp.square(jnp.maximum(u, 0.0)).astype(BF16)


def _mlp_down_kernel(x_ref, u_ref, wd_ref, fg_ref, y_ref, *, final_norm):
    y = x_ref[...] + jnp.dot(u_ref[...], wd_ref[...], preferred_element_type=F32)
    if final_norm:
        y = _rms(y, fg_ref[...])
    y_ref[...] = y


def _mlp(x2, norm_g, wu, wd, final_g, final_norm, tm, tf):
    T = x2.shape[0]
    whole = lambda a: pl.BlockSpec(a.shape, lambda i: (0,) * a.ndim)
    rows = lambda width: pl.BlockSpec((tm, width), lambda i: (i, 0))
    u = pl.pallas_call(
        functools.partial(_mlp_up_kernel, tf=tf),
        grid=(T // tm,),
        in_specs=[rows(D_MODEL), whole(norm_g), whole(wu)],
        out_specs=rows(D_FF),
        out_shape=jax.ShapeDtypeStruct((T, D_FF), BF16),
        scratch_shapes=[pltpu.VMEM((tm, D_MODEL), BF16)],
        compiler_params=_params(("parallel",)),
    )(x2, norm_g, wu)
    return pl.pallas_call(
        functools.partial(_mlp_down_kernel, final_norm=final_norm),
        grid=(T // tm,),
        in_specs=[rows(D_MODEL), rows(D_FF), whole(wd), whole(final_g)],
        out_specs=rows(D_MODEL),
        out_shape=jax.ShapeDtypeStruct((T, D_MODEL), F32),
        compiler_params=_params(("parallel",)),
    )(x2, u, wd, final_g)


def _rotary_lane_order(w):
    split = ROT_DIM + DH // 2 - ROT_HALF
    return jnp.concatenate([w[..., :ROT_HALF], w[..., ROT_DIM:split],
                            w[..., ROT_HALF:ROT_DIM], w[..., split:]], axis=-1)


def _rot_tables(seq_len):
    pos = jnp.arange(seq_len, dtype=F32)
    inv_freq = 1.0 / (jnp.float32(ROPE_THETA) ** (jnp.arange(0, ROT_DIM, 2, dtype=F32) / ROT_DIM))
    ang = pos[:, None] * inv_freq[None, :]
    cos, sin = jnp.cos(ang), jnp.sin(ang)
    pad = DH // 2 - ROT_HALF
    ones, zeros = jnp.ones((seq_len, pad), F32), jnp.zeros((seq_len, pad), F32)
    rc = jnp.concatenate([cos, ones, cos, ones], axis=-1)
    rs = jnp.concatenate([-sin, zeros, sin, zeros], axis=-1)
    return rc * EXP2_SCALE, rs * EXP2_SCALE, rc, rs


def _token_tile(seq_len):
    return min(512, seq_len // 2)


def _prepare(p):
    row = lambda a: a.reshape(1, -1).astype(F32)
    layers = []
    for l in range(p["w_in"].shape[0]):
        w_in = p["w_in"][l]
        w_qk = _rotary_lane_order(w_in[:, :2 * ATTN_W].reshape(D_MODEL, -1, DH))
        w_proj = jnp.concatenate([w_qk.reshape(D_MODEL, 2 * ATTN_W),
                                  w_in[:, 2 * ATTN_W:PROJ_COLS]], axis=1).astype(BF16)
        layers.append(dict(
            attn_norm_g=row(p["attn_norm_g"][l]), w_proj=w_proj,
            w_gates=w_in[:, PROJ_COLS:].astype(BF16),
            lq1=row(p["lambda_q1"][l]), lk1=row(p["lambda_k1"][l]),
            lq2=row(p["lambda_q2"][l]), lk2=row(p["lambda_k2"][l]),
            subln_g=row(p["subln_g"][l]), ln_g=row(p["sgu_ln_g"][l]), ln_b=row(p["sgu_ln_b"][l]),
            sgu_w=p["sgu_w"][l].astype(BF16), sgu_bt=p["sgu_b"][l].T.astype(F32),
            w_br_attn=p["w_br_attn"][l].astype(BF16), w_br_sgu=p["w_br_sgu"][l].astype(BF16),
            w_out=p["w_out"][l].astype(BF16), mlp_norm_g=row(p["mlp_norm_g"][l]),
            w_up=p["w_up"][l].astype(BF16), w_down=p["w_down"][l].astype(BF16)))
    return layers, row(p["final_norm_g"])


def _trunk(x, layers, final_g):
    B, S, _ = x.shape
    tm = _token_tile(S)
    rot = _rot_tables(S)
    x2 = x.reshape(B * S, D_MODEL)
    for l, w in enumerate(layers):
        qt4, vt4, z = _in_proj(x2, B, S, w["attn_norm_g"], w["w_proj"], rot, w["ln_g"], w["ln_b"],
                               w["sgu_w"], w["sgu_bt"], tm)
        oa = _attention(qt4, z.reshape(B, S, ZCOLS), vt4, w["lq1"], w["lk1"], w["lq2"], w["lk2"],
                        w["subln_g"], _lambda_init(l))
        x2 = _merge(x2, z, oa.reshape(B * S, ATTN_W), w["attn_norm_g"], w["w_gates"],
                    w["w_br_attn"], w["w_br_sgu"], w["w_out"], MERGE_TM, MERGE_TN)
        x2 = _mlp(x2, w["mlp_norm_g"], w["w_up"], w["w_down"], final_g,
                  l == len(layers) - 1, MLP_TM, MLP_TF)
    return x2.reshape(B, S, D_MODEL)


def kernel(x_prompt, x_sample, attn_norm_g, w_in, lambda_q1, lambda_k1, lambda_q2, lambda_k2,
           subln_g, sgu_ln_g, sgu_ln_b, sgu_w, sgu_b, w_br_attn, w_br_sgu, w_out, mlp_norm_g,
           w_up, w_down, final_norm_g):
    p = dict(attn_norm_g=attn_norm_g, w_in=w_in, lambda_q1=lambda_q1, lambda_k1=lambda_k1,
             lambda_q2=lambda_q2, lambda_k2=lambda_k2, subln_g=subln_g, sgu_ln_g=sgu_ln_g,
             sgu_ln_b=sgu_ln_b, sgu_w=sgu_w, sgu_b=sgu_b, w_br_attn=w_br_attn, w_br_sgu=w_br_sgu,
             w_out=w_out, mlp_norm_g=mlp_norm_g, w_up=w_up, w_down=w_down,
             final_norm_g=final_norm_g)
    layers, final_g = _prepare(p)
    return _trunk(x_prompt, layers, final_g), _trunk(x_sample, layers, final_g)
```
